```python
import jax, jax.numpy as jnp
from jax import lax
import numpy as np

D_MODEL = 4096
BATCH = 2
SEQ = 4096
DEPTH = 2

MIX_WIDTH = D_MODEL
MEM_LEN = 256
D_FF = ((8 * D_MODEL // 3 + 255) // 256) * 256
N_SUB = 4
DN_ALPHA = (2 * DEPTH) ** 0.25
DN_BETA = (8 * DEPTH) ** -0.25
N_EVEN = (DEPTH + 1) // 2
N_ODD = DEPTH // 2

MLA_V_DIM = 128
MLA_NOPE_DIM = 128
MLA_ROPE_DIM = 64
MLA_HEADS = (MIX_WIDTH // 2) // MLA_V_DIM
MLA_Q_LORA = D_MODEL // 4
MLA_KV_LORA = D_MODEL // 8
ROPE_THETA = 10000.0
ATTN_BLOCK = 128

GLA_HEADS = 4
GLA_DV = (MIX_WIDTH // 2) // GLA_HEADS
GLA_DK = GLA_DV // 2
GLA_GATE_RANK = 16
GLA_GATE_NORM = 16.0

GDN_DK = 128
GDN_DV = 128
GDN_V_HEADS = MIX_WIDTH // GDN_DV
GDN_K_HEADS = GDN_V_HEADS // 2
CONV_K = 4

XA_HEADS = 4
XA_DIM = 128

CHUNK = 64

AB_SIZES = (MLA_Q_LORA, MLA_KV_LORA, MLA_ROPE_DIM,
            GLA_HEADS * GLA_DK, GLA_HEADS * GLA_DK, GLA_HEADS * GLA_DV,
            GLA_GATE_RANK, GLA_HEADS * GLA_DV)
AB_IN = sum(AB_SIZES)
AB_OUT = MLA_HEADS * MLA_V_DIM + GLA_HEADS * GLA_DV
GDN_QK = GDN_K_HEADS * GDN_DK
GDN_VW = GDN_V_HEADS * GDN_DV
GDN_CONV_CH = 2 * GDN_QK + GDN_VW
C_SIZES = (GDN_CONV_CH, GDN_VW, GDN_V_HEADS, GDN_V_HEADS)
C_IN = sum(C_SIZES)

kernel_name = "hybrid_mla_gla_gdn_deepnorm_macaron"


def _split(h, sizes):
    out, off = [], 0
    for s in sizes:
        out.append(h[..., off:off + s])
        off += s
    return out


def layer_norm(x, g, b, eps=1e-5):
    xf = x.astype(jnp.float32)
    mu = jnp.mean(xf, axis=-1, keepdims=True)
    xc = xf - mu
    var = jnp.mean(xc * xc, axis=-1, keepdims=True)
    return (xc * lax.rsqrt(var + eps) * g.astype(jnp.float32) + b.astype(jnp.float32)).astype(x.dtype)


def rms_norm(x, g, eps=1e-6):
    xf = x.astype(jnp.float32)
    y = xf * lax.rsqrt(jnp.mean(xf * xf, axis=-1, keepdims=True) + eps)
    return (y * g.astype(jnp.float32)).astype(x.dtype)


def l2_norm(x, eps=1e-6):
    xf = x.astype(jnp.float32)
    return xf * lax.rsqrt(jnp.sum(xf * xf, axis=-1, keepdims=True) + eps)


def rope_tables(positions):
    inv = 1.0 / (ROPE_THETA ** (jnp.arange(0, MLA_ROPE_DIM, 2, dtype=jnp.float32) / MLA_ROPE_DIM))
    ang = positions.astype(jnp.float32)[..., None] * inv
    return jnp.cos(ang), jnp.sin(ang)


def apply_rope(x, cos, sin):
    half = x.shape[-1] // 2
    xf = x.astype(jnp.float32)
    x1, x2 = xf[..., :half], xf[..., half:]
    return jnp.concatenate([x1 * cos - x2 * sin, x2 * cos + x1 * sin], axis=-1).astype(x.dtype)


def swiglu(x, wg, wu, wd):
    return (jax.nn.silu(x @ wg) * (x @ wu)) @ wd


def causal_softmax_attention(q, k, v):
    B, H, S, Dq = q.shape
    nb = S // ATTN_BLOCK
    qb = jnp.moveaxis(q.reshape(B, H, nb, ATTN_BLOCK, Dq), 2, 0)
    kpos = jnp.arange(S)
    scale = Dq ** -0.5

    def block(args):
        q_blk, i = args
        s = jnp.einsum('bhqd,bhkd->bhqk', q_blk, k, preferred_element_type=jnp.float32) * scale
        qpos = i * ATTN_BLOCK + jnp.arange(ATTN_BLOCK)
        s = jnp.where(kpos[None, :] <= qpos[:, None], s, -jnp.inf)
        p = jax.nn.softmax(s, axis=-1).astype(v.dtype)
        return jnp.einsum('bhqk,bhkd->bhqd', p, v)

    o = lax.map(block, (qb, jnp.arange(nb)))
    return jnp.moveaxis(o, 0, 2).reshape(B, H, S, v.shape[-1])


def gla_chunked(q, k, v, log_a):
    B, H, S, dk = q.shape
    dv = v.shape[-1]
    n = S // CHUNK
    q = q * dk ** -0.5
    q, k, log_a = [t.reshape(B, H, n, CHUNK, dk) for t in (q, k, log_a)]
    v = v.reshape(B, H, n, CHUNK, dv)
    b = jnp.cumsum(log_a, axis=3)
    b_last = b[:, :, :, -1:, :]
    q_in = q * jnp.exp(b)
    k_in = k * jnp.exp(-b)
    k_tail = k * jnp.exp(b_last - b)
    causal = jnp.tril(jnp.ones((CHUNK, CHUNK), dtype=bool))
    attn = jnp.where(causal, jnp.einsum('bhncd,bhnjd->bhncj', q_in, k_in), 0.0)
    o_intra = jnp.einsum('bhncj,bhnjv->bhncv', attn, v)

    def step(state, inp):
        q_c, kt_c, v_c, dec_c = inp
        o_c = jnp.einsum('bhcd,bhdv->bhcv', q_c, state)
        state = dec_c[..., None] * state + jnp.einsum('bhcd,bhcv->bhdv', kt_c, v_c)
        return state, o_c

    xs = tuple(jnp.moveaxis(t, 2, 0) for t in (q_in, k_tail, v, jnp.exp(b_last[:, :, :, 0, :])))
    s0 = jnp.zeros((B, H, dk, dv), jnp.float32)
    _, o_inter = lax.scan(step, s0, xs)
    o = jnp.moveaxis(o_inter, 0, 2) + o_intra
    return o.reshape(B, H, S, dv)


def gated_delta_chunked(q, k, v, g, beta):
    B, H, S, dk = q.shape
    dv = v.shape[-1]
    n = S // CHUNK
    q = q * dk ** -0.5
    q, k = [t.reshape(B, H, n, CHUNK, dk) for t in (q, k)]
    v = v.reshape(B, H, n, CHUNK, dv)
    g = g.reshape(B, H, n, CHUNK)
    beta = beta.reshape(B, H, n, CHUNK)
    gc = jnp.cumsum(g, axis=-1)
    incl = jnp.tril(jnp.ones((CHUNK, CHUNK), dtype=bool))
    strict = jnp.tril(jnp.ones((CHUNK, CHUNK), dtype=bool), -1)
    decay = jnp.exp(jnp.where(incl, gc[..., :, None] - gc[..., None, :], -jnp.inf))
    kb = k * beta[..., None]
    m = jnp.where(strict, jnp.einsum('bhncd,bhnjd->bhncj', kb, k) * decay, 0.0)
    t_mat = m + jnp.eye(CHUNK, dtype=m.dtype)
    u = lax.linalg.triangular_solve(t_mat, v * beta[..., None], left_side=True, lower=True,
                                    unit_diagonal=True)
    w = lax.linalg.triangular_solve(t_mat, kb * jnp.exp(gc)[..., None], left_side=True, lower=True,
                                    unit_diagonal=True)
    attn = jnp.einsum('bhncd,bhnjd->bhncj', q, k) * decay
    q_g = q * jnp.exp(gc)[..., None]
    k_tail = k * jnp.exp(gc[..., -1:] - gc)[..., None]
    g_last = jnp.exp(gc[..., -1])

    def step(state, inp):
        qg_c, w_c, u_c, a_c, kt_c, gl_c = inp
        v_new = u_c - jnp.einsum('bhcd,bhdv->bhcv', w_c, state)
        o_c = jnp.einsum('bhcd,bhdv->bhcv', qg_c, state) + jnp.einsum('bhcj,bhjv->bhcv', a_c, v_new)
        state = gl_c[..., None, None] * state + jnp.einsum('bhcd,bhcv->bhdv', kt_c, v_new)
        return state, o_c

    xs = tuple(jnp.moveaxis(t, 2, 0) for t in (q_g, w, u, attn, k_tail, g_last))
    s0 = jnp.zeros((B, H, dk, dv), jnp.float32)
    _, o = lax.scan(step, s0, xs)
    return jnp.moveaxis(o, 0, 2).reshape(B, H, S, dv)


def causal_depthwise_conv(x, w):
    return lax.conv_general_dilated(x, w[:, None, :].astype(x.dtype), window_strides=(1,),
                                    padding=[(CONV_K - 1, 0)],
                                    dimension_numbers=('NWC', 'WIO', 'NWC'),
                                    feature_group_count=x.shape[-1])


def mixer_mla_gla(x, cos, sin, w_in, q_norm, w_uq, kv_norm, w_ukv, gla_w_gu, gla_b_g, gla_norm, w_out):
    B, S, _ = x.shape
    h = x @ w_in
    c_q, c_kv, k_pe, gq, gk, gv, g_lr, r = _split(h, AB_SIZES)
    q = (rms_norm(c_q, q_norm) @ w_uq).reshape(B, S, MLA_HEADS, MLA_NOPE_DIM + MLA_ROPE_DIM)
    q_nope = q[..., :MLA_NOPE_DIM]
    q_pe = apply_rope(q[..., MLA_NOPE_DIM:], cos[:, :, None, :], sin[:, :, None, :])
    kv = (rms_norm(c_kv, kv_norm) @ w_ukv).reshape(B, S, MLA_HEADS, MLA_NOPE_DIM + MLA_V_DIM)
    k_nope, v = kv[..., :MLA_NOPE_DIM], kv[..., MLA_NOPE_DIM:]
    k_pe = apply_rope(k_pe, cos, sin)
    qh = jnp.concatenate([q_nope, q_pe], axis=-1).transpose(0, 2, 1, 3)
    kh = jnp.concatenate([k_nope, jnp.broadcast_to(k_pe[:, :, None, :], (B, S, MLA_HEADS, MLA_ROPE_DIM))],
                         axis=-1).transpose(0, 2, 1, 3)
    o_mla = causal_softmax_attention(qh, kh, v.transpose(0, 2, 1, 3))
    o_mla = o_mla.transpose(0, 2, 1, 3).reshape(B, S, MLA_HEADS * MLA_V_DIM)
    log_a = jax.nn.log_sigmoid((g_lr @ gla_w_gu + gla_b_g).astype(jnp.float32)) / GLA_GATE_NORM

    def heads(t, d):
        return t.reshape(B, S, GLA_HEADS, d).transpose(0, 2, 1, 3).astype(jnp.float32)

    o_gla = gla_chunked(heads(gq, GLA_DK), heads(gk, GLA_DK), heads(gv, GLA_DV), heads(log_a, GLA_DK))
    o_gla = rms_norm(o_gla.transpose(0, 2, 1, 3), gla_norm) * jax.nn.silu(
        r.reshape(B, S, GLA_HEADS, GLA_DV).astype(jnp.float32))
    o_gla = o_gla.reshape(B, S, GLA_HEADS * GLA_DV).astype(x.dtype)
    return jnp.concatenate([o_mla.astype(x.dtype), o_gla], axis=-1) @ w_out


def mixer_gdn(x, w_in, conv_w, a_log, dt_bias, norm_g, w_out):
    B, S, _ = x.shape
    h = x @ w_in
    qkv, z, b, a = _split(h, C_SIZES)
    qkv = jax.nn.silu(causal_depthwise_conv(qkv, conv_w))
    q, k, v = _split(qkv, (GDN_QK, GDN_QK, GDN_VW))
    rep = GDN_V_HEADS // GDN_K_HEADS
    q = jnp.repeat(l2_norm(q.reshape(B, S, GDN_K_HEADS, GDN_DK)), rep, axis=2)
    k = jnp.repeat(l2_norm(k.reshape(B, S, GDN_K_HEADS, GDN_DK)), rep, axis=2)
    v = v.reshape(B, S, GDN_V_HEADS, GDN_DV).astype(jnp.float32)
    beta = jax.nn.sigmoid(b.astype(jnp.float32))
    g = -jnp.exp(a_log.astype(jnp.float32)) * jax.nn.softplus(a.astype(jnp.float32) + dt_bias.astype(jnp.float32))
    o = gated_delta_chunked(q.transpose(0, 2, 1, 3), k.transpose(0, 2, 1, 3), v.transpose(0, 2, 1, 3),
                            g.transpose(0, 2, 1), beta.transpose(0, 2, 1))
    o = rms_norm(o.transpose(0, 2, 1, 3), norm_g) * jax.nn.silu(
        z.reshape(B, S, GDN_V_HEADS, GDN_DV).astype(jnp.float32))
    return o.reshape(B, S, GDN_VW).astype(x.dtype) @ w_out


def cross_attention(x, memn, wq, wkv, wo):
    B, S, _ = x.shape
    M = memn.shape[1]
    q = (x @ wq).reshape(B, S, XA_HEADS, XA_DIM)
    kv = (memn @ wkv).reshape(B, M, 2, XA_HEADS, XA_DIM)
    k, v = kv[:, :, 0], kv[:, :, 1]
    s = jnp.einsum('bshd,bmhd->bhsm', q, k, preferred_element_type=jnp.float32) * XA_DIM ** -0.5
    p = jax.nn.softmax(s, axis=-1).astype(v.dtype)
    o = jnp.einsum('bhsm,bmhd->bshd', p, v).reshape(B, S, XA_HEADS * XA_DIM)
    return o @ wo


def setup_inputs(seed: int = 0) -> dict:
    key = jax.random.key(seed)
    ks = iter(jax.random.split(key, 40))

    def nrm(shape, fan_in, scale=1.0):
        return jax.random.normal(next(ks), shape, jnp.float32) * (scale * fan_in ** -0.5)

    def gain(shape):
        return 1.0 + 0.02 * jax.random.normal(next(ks), shape, jnp.float32)

    def small(shape):
        return 0.02 * jax.random.normal(next(ks), shape, jnp.float32)

    x = jax.random.normal(next(ks), (BATCH, SEQ, D_MODEL), jnp.float32)
    mem = jax.random.normal(next(ks), (BATCH, MEM_LEN, D_MODEL), jnp.float32)
    offset = jax.random.randint(next(ks), (BATCH, 1), 0, 1024, dtype=jnp.int32)
    positions = (offset + jnp.arange(SEQ, dtype=jnp.int32)[None, :]).astype(jnp.int32)
    dt = jnp.exp(jax.random.uniform(next(ks), (N_ODD, GDN_V_HEADS), jnp.float32,
                                    minval=float(np.log(1e-3)), maxval=float(np.log(1e-1))))
    return {
        "x": x,
        "mem": mem,
        "positions": positions,
        "mem_ln_g": gain((D_MODEL,)),
        "mem_ln_b": small((D_MODEL,)),
        "ln_g": gain((DEPTH, N_SUB, D_MODEL)),
        "ln_b": small((DEPTH, N_SUB, D_MODEL)),
        "ffn_w_gate": nrm((DEPTH, 2, D_MODEL, D_FF), D_MODEL),
        "ffn_w_up": nrm((DEPTH, 2, D_MODEL, D_FF), D_MODEL),
        "ffn_w_down": nrm((DEPTH, 2, D_FF, D_MODEL), D_FF, DN_BETA),
        "ab_w_in": nrm((N_EVEN, D_MODEL, AB_IN), D_MODEL),
        "mla_q_norm": gain((N_EVEN, MLA_Q_LORA)),
        "mla_w_uq": nrm((N_EVEN, MLA_Q_LORA, MLA_HEADS * (MLA_NOPE_DIM + MLA_ROPE_DIM)), MLA_Q_LORA),
        "mla_kv_norm": gain((N_EVEN, MLA_KV_LORA)),
        "mla_w_ukv": nrm((N_EVEN, MLA_KV_LORA, MLA_HEADS * (MLA_NOPE_DIM + MLA_V_DIM)), MLA_KV_LORA),
        "gla_w_gate_up": nrm((N_EVEN, GLA_GATE_RANK, GLA_HEADS * GLA_DK), GLA_GATE_RANK),
        "gla_b_gate": small((N_EVEN, GLA_HEADS * GLA_DK)),
        "gla_norm": gain((N_EVEN, GLA_DV)),
        "ab_w_out": nrm((N_EVEN, AB_OUT, D_MODEL), AB_OUT, DN_BETA),
        "c_w_in": nrm((N_ODD, D_MODEL, C_IN), D_MODEL),
        "gdn_conv_w": nrm((N_ODD, CONV_K, GDN_CONV_CH), CONV_K),
        "gdn_a_log": jnp.log(jax.random.uniform(next(ks), (N_ODD, GDN_V_HEADS), jnp.float32,
                                                minval=1.0, maxval=16.0)),
        "gdn_dt_bias": dt + jnp.log(-jnp.expm1(-dt)),
        "gdn_norm": gain((N_ODD, GDN_DV)),
        "c_w_out": nrm((N_ODD, GDN_VW, D_MODEL), GDN_VW, DN_BETA),
        "xa_wq": nrm((DEPTH, D_MODEL, XA_HEADS * XA_DIM), D_MODEL),
        "xa_wkv": nrm((DEPTH, D_MODEL, 2 * XA_HEADS * XA_DIM), D_MODEL),
        "xa_wo": nrm((DEPTH, XA_HEADS * XA_DIM, D_MODEL), XA_HEADS * XA_DIM, DN_BETA),
    }


def reference(x, mem, positions, mem_ln_g, mem_ln_b, ln_g, ln_b, ffn_w_gate, ffn_w_up, ffn_w_down,
              ab_w_in, mla_q_norm, mla_w_uq, mla_kv_norm, mla_w_ukv, gla_w_gate_up, gla_b_gate, gla_norm,
              ab_w_out, c_w_in, gdn_conv_w, gdn_a_log, gdn_dt_bias, gdn_norm, c_w_out,
              xa_wq, xa_wkv, xa_wo):
    cos, sin = rope_tables(positions)
    memn = layer_norm(mem, mem_ln_g, mem_ln_b)
    for l in range(DEPTH):
        i = l // 2
        x = layer_norm(DN_ALPHA * x + 0.5 * swiglu(x, ffn_w_gate[l, 0], ffn_w_up[l, 0], ffn_w_down[l, 0]),
                       ln_g[l, 0], ln_b[l, 0])
        if l % 2 == 0:
            y = mixer_mla_gla(x, cos, sin, ab_w_in[i], mla_q_norm[i], mla_w_uq[i], mla_kv_norm[i],
                              mla_w_ukv[i], gla_w_gate_up[i], gla_b_gate[i], gla_norm[i], ab_w_out[i])
        else:
            y = mixer_gdn(x, c_w_in[i], gdn_conv_w[i], gdn_a_log[i], gdn_dt_bias[i], gdn_norm[i], c_w_out[i])
        x = layer_norm(DN_ALPHA * x + y, ln_g[l, 1], ln_b[l, 1])
        x = layer_norm(DN_ALPHA * x + cross_attention(x, memn, xa_wq[l], xa_wkv[l], xa_wo[l]),
                       ln_g[l, 2], ln_b[l, 2])
        x = layer_norm(DN_ALPHA * x + 0.5 * swiglu(x, ffn_w_gate[l, 1], ffn_w_up[l, 1], ffn_w_down[l, 1]),
                       ln_g[l, 3], ln_b[l, 3])
    return x
```

```python
import functools

import jax
import jax.numpy as jnp
from jax import lax
from jax.experimental import pallas as pl
from jax.experimental.pallas import tpu as pltpu

F32 = jnp.float32
BF16 = jnp.bfloat16

VMEM_LIMIT_BYTES = 56 * 1024 * 1024
LANES = 128

DEPTH = 2
DN_ALPHA = (2 * DEPTH) ** 0.25
ROPE_THETA = 10000.0
CHUNK = 64
GLA_GATE_NORM = 16.0
LN_EPS = 1e-5
RMS_EPS = 1e-6
L2_EPS = 1e-6
LN_ROWS = 64

NT = (((1,), (1,)), ((), ()))
TN = (((0,), (0,)), ((), ()))


def _cparams(*sem):
    return pltpu.CompilerParams(dimension_semantics=sem, vmem_limit_bytes=VMEM_LIMIT_BYTES)


def _dot(a, b, dims=None):
    a = a.astype(BF16)
    b = b.astype(BF16)
    if dims is None:
        return jnp.dot(a, b, preferred_element_type=F32)
    return lax.dot_general(a, b, dims, preferred_element_type=F32)


def _split3(x):
    hi = x.astype(BF16)
    r = x - hi.astype(F32)
    mid = r.astype(BF16)
    lo = (r - mid.astype(F32)).astype(BF16)
    return hi, mid, lo


def _dot_mask(mask_bf16, x, dims=None, mask_first=True):
    out = None
    for piece in _split3(x):
        if mask_first:
            p = (jnp.dot(mask_bf16, piece, preferred_element_type=F32) if dims is None
                 else lax.dot_general(mask_bf16, piece, dims, preferred_element_type=F32))
        else:
            p = (jnp.dot(piece, mask_bf16, preferred_element_type=F32) if dims is None
                 else lax.dot_general(piece, mask_bf16, dims, preferred_element_type=F32))
        out = p if out is None else out + p
    return out


def _dot_hilo(a, b):
    a_hi = a.astype(BF16)
    a_lo = (a - a_hi.astype(F32)).astype(BF16)
    b_hi = b.astype(BF16)
    b_lo = (b - b_hi.astype(F32)).astype(BF16)
    return (jnp.dot(a_hi, b_hi, preferred_element_type=F32)
            + jnp.dot(a_hi, b_lo, preferred_element_type=F32)
            + jnp.dot(a_lo, b_hi, preferred_element_type=F32))


def _silu(x):
    return x * jax.nn.sigmoid(x)


def _layer_norm(y, g, b):
    mu = jnp.mean(y, axis=-1, keepdims=True)
    yc = y - mu
    var = jnp.mean(yc * yc, axis=-1, keepdims=True)
    return yc * lax.rsqrt(var + LN_EPS) * g + b


def _rms_norm(y, g):
    return y * lax.rsqrt(jnp.mean(y * y, axis=-1, keepdims=True) + RMS_EPS) * g


def _pick(n, prefs):
    for p in prefs:
        if n % p == 0:
            return p
    return n


def _ln_rows_kernel(x_ref, g_ref, b_ref, ob_ref):
    ob_ref[...] = _layer_norm(x_ref[...], g_ref[...], b_ref[...]).astype(ob_ref.dtype)


def ln_rows(x, g, b):
    m, d = x.shape
    tm = _pick(m, (256, 128, 64, 32, 16))
    return pl.pallas_call(
        _ln_rows_kernel,
        grid=(m // tm,),
        in_specs=[pl.BlockSpec((tm, d), lambda i: (i, 0)),
                  pl.BlockSpec((1, d), lambda i: (0, 0)),
                  pl.BlockSpec((1, d), lambda i: (0, 0))],
        out_specs=pl.BlockSpec((tm, d), lambda i: (i, 0)),
        out_shape=jax.ShapeDtypeStruct((m, d), BF16),
        compiler_params=_cparams("arbitrary"),
        name="ln_rows",
    )(x, g.reshape(1, d), b.reshape(1, d))


def _gateup_kernel(x_ref, wg_ref, wu_ref, h_ref, wgb_ref, wub_ref):
    @pl.when(pl.program_id(1) == 0)
    def _():
        wgb_ref[...] = wg_ref[...].astype(BF16)
        wub_ref[...] = wu_ref[...].astype(BF16)

    x = x_ref[...]
    g = jnp.dot(x, wgb_ref[...], preferred_element_type=F32)
    u = jnp.dot(x, wub_ref[...], preferred_element_type=F32)
    h_ref[...] = (_silu(g) * u).astype(h_ref.dtype)


def ffn_gateup(xb, wg_all, wu_all, l, s):
    t, d = xb.shape
    f = wg_all.shape[-1]
    tm = _pick(t, (1024, 512, 256, 128))
    tn = _pick(f, (256, 128))
    wspec = pl.BlockSpec((None, None, d, tn), lambda j, i: (l, s, 0, j))
    return pl.pallas_call(
        _gateup_kernel,
        grid=(f // tn, t // tm),
        in_specs=[pl.BlockSpec((tm, d), lambda j, i: (i, 0)), wspec, wspec],
        out_specs=pl.BlockSpec((tm, tn), lambda j, i: (i, j)),
        out_shape=jax.ShapeDtypeStruct((t, f), BF16),
        scratch_shapes=[pltpu.VMEM((d, tn), BF16), pltpu.VMEM((d, tn), BF16)],
        compiler_params=_cparams("arbitrary", "arbitrary"),
        name="ffn_gateup",
    )(xb, wg_all, wu_all)


def _down_ln_kernel(a_ref, w_ref, r_ref, g_ref, b_ref, o_ref, ob_ref, *, scale, nk):
    k = pl.program_id(1)
    p = jnp.dot(a_ref[...], w_ref[...], preferred_element_type=F32)

    @pl.when(k == 0)
    def _():
        o_ref[...] = p

    @pl.when(k > 0)
    def _():
        o_ref[...] += p

    @pl.when(k == nk - 1)
    def _():
        g = g_ref[...]
        b = b_ref[...]
        rc = min(LN_ROWS, o_ref.shape[0])

        def body(r, carry):
            rows = pl.ds(pl.multiple_of(r * rc, rc), rc)
            y = DN_ALPHA * r_ref[rows, :] + scale * o_ref[rows, :]
            out = _layer_norm(y, g, b)
            o_ref[rows, :] = out
            ob_ref[rows, :] = out.astype(ob_ref.dtype)
            return carry

        lax.fori_loop(0, o_ref.shape[0] // rc, body, 0)


def down_ln(a, w, resid, g, b, scale):
    t, kdim = a.shape
    d = w.shape[1]
    tm = _pick(t, (512, 256, 128))
    tk = _pick(kdim, (512, 256, 128))
    nk = kdim // tk
    return pl.pallas_call(
        functools.partial(_down_ln_kernel, scale=scale, nk=nk),
        grid=(t // tm, nk),
        in_specs=[pl.BlockSpec((tm, tk), lambda i, k: (i, k)),
                  pl.BlockSpec((tk, d), lambda i, k: (k, 0)),
                  pl.BlockSpec((tm, d), lambda i, k: (i, 0), pipeline_mode=pl.Buffered(1)),
                  pl.BlockSpec((1, d), lambda i, k: (0, 0)),
                  pl.BlockSpec((1, d), lambda i, k: (0, 0))],
        out_specs=[pl.BlockSpec((tm, d), lambda i, k: (i, 0)),
                   pl.BlockSpec((tm, d), lambda i, k: (i, 0))],
        out_shape=[jax.ShapeDtypeStruct((t, d), F32), jax.ShapeDtypeStruct((t, d), BF16)],
        compiler_params=_cparams("arbitrary", "arbitrary"),
        name="down_ln",
    )(a, w, resid, g.reshape(1, d), b.reshape(1, d))


def _proj_kernel(a_ref, w_ref, o_ref):
    o_ref[...] = jnp.dot(a_ref[...], w_ref[...], preferred_element_type=F32).astype(o_ref.dtype)


def proj(a, w, out_dtype=F32):
    t, kdim = a.shape
    n = w.shape[1]
    tm = _pick(t, (1024, 512, 256, 128))
    tn = _pick(n, (1024, 768, 512, 384, 256, 128))
    return pl.pallas_call(
        _proj_kernel,
        grid=(t // tm, n // tn),
        in_specs=[pl.BlockSpec((tm, kdim), lambda i, j: (i, 0)),
                  pl.BlockSpec((kdim, tn), lambda i, j: (0, j))],
        out_specs=pl.BlockSpec((tm, tn), lambda i, j: (i, j)),
        out_shape=jax.ShapeDtypeStruct((t, n), out_dtype),
        compiler_params=_cparams("arbitrary", "arbitrary"),
        name="proj",
    )(a, w)


def _rope128(v, c, s):
    return v * c + pltpu.roll(v, 64, axis=1) * s


def _mla_q_kernel(c_ref, gn_ref, w_ref, cos_ref, sin_ref, q_ref, *, heads_per_tile, scale):
    n = _rms_norm(c_ref[...], gn_ref[...]).astype(BF16)
    q = jnp.dot(n, w_ref[...], preferred_element_type=F32) * scale
    c = cos_ref[...]
    s = sin_ref[...]
    for h in range(heads_per_tile):
        o = h * 2 * LANES
        q_ref[:, o:o + LANES] = q[:, o:o + LANES].astype(q_ref.dtype)
        q_ref[:, o + LANES:o + 2 * LANES] = _rope128(q[:, o + LANES:o + 2 * LANES], c, s).astype(q_ref.dtype)


def mla_q_proj(h_main, col_block, rank, gnorm, w, cos_t, sin_t, scale):
    t = h_main.shape[0]
    n = w.shape[1]
    tm = _pick(t, (512, 256, 128))
    tn = _pick(n, (1024, 512, 256))
    return pl.pallas_call(
        functools.partial(_mla_q_kernel, heads_per_tile=tn // (2 * LANES), scale=scale),
        grid=(t // tm, n // tn),
        in_specs=[pl.BlockSpec((tm, rank), lambda i, j: (i, col_block)),
                  pl.BlockSpec((1, rank), lambda i, j: (0, 0)),
                  pl.BlockSpec((rank, tn), lambda i, j: (0, j)),
                  pl.BlockSpec((tm, LANES), lambda i, j: (i, 0)),
                  pl.BlockSpec((tm, LANES), lambda i, j: (i, 0))],
        out_specs=pl.BlockSpec((tm, tn), lambda i, j: (i, j)),
        out_shape=jax.ShapeDtypeStruct((t, n), BF16),
        compiler_params=_cparams("arbitrary", "arbitrary"),
        name="mla_q_proj",
    )(h_main, gnorm.reshape(1, rank), w, cos_t, sin_t)


def _mla_kv_kernel(c_ref, gn_ref, w_ref, o_ref):
    n = _rms_norm(c_ref[...], gn_ref[...]).astype(BF16)
    o_ref[...] = jnp.dot(n, w_ref[...], preferred_element_type=F32).astype(o_ref.dtype)


def mla_kv_proj(h_main, col_block, rank, gnorm, w):
    t = h_main.shape[0]
    n = w.shape[1]
    tm = _pick(t, (512, 256, 128))
    tn = _pick(n, (1024, 512, 256, 128))
    return pl.pallas_call(
        _mla_kv_kernel,
        grid=(t // tm, n // tn),
        in_specs=[pl.BlockSpec((tm, rank), lambda i, j: (i, col_block)),
                  pl.BlockSpec((1, rank), lambda i, j: (0, 0)),
                  pl.BlockSpec((rank, tn), lambda i, j: (0, j))],
        out_specs=pl.BlockSpec((tm, tn), lambda i, j: (i, j)),
        out_shape=jax.ShapeDtypeStruct((t, n), BF16),
        compiler_params=_cparams("arbitrary", "arbitrary"),
        name="mla_kv_proj",
    )(h_main, gnorm.reshape(1, rank), w)


def _ab_small_kernel(hs_ref, cos_ref, sin_ref, wgu_ref, bg_ref, kpe_ref, la_ref):
    hs = hs_ref[...]
    kpe_ref[...] = _rope128(hs[:, :LANES], cos_ref[...], sin_ref[...]).astype(kpe_ref.dtype)
    z = jnp.dot(hs[:, LANES:].astype(BF16), wgu_ref[...], preferred_element_type=F32) + bg_ref[...]
    log_sig = jnp.minimum(z, 0.0) - jnp.log1p(jnp.exp(-jnp.abs(z)))
    la_ref[...] = log_sig / GLA_GATE_NORM


def ab_small(hs, cos_t, sin_t, wgu_pad, b_gate):
    t = hs.shape[0]
    n = wgu_pad.shape[1]
    tm = _pick(t, (512, 256, 128))
    return pl.pallas_call(
        _ab_small_kernel,
        grid=(t // tm,),
        in_specs=[pl.BlockSpec((tm, 2 * LANES), lambda i: (i, 0)),
                  pl.BlockSpec((tm, LANES), lambda i: (i, 0)),
                  pl.BlockSpec((tm, LANES), lambda i: (i, 0)),
                  pl.BlockSpec((LANES, n), lambda i: (0, 0)),
                  pl.BlockSpec((1, n), lambda i: (0, 0))],
        out_specs=[pl.BlockSpec((tm, LANES), lambda i: (i, 0)),
                   pl.BlockSpec((tm, n), lambda i: (i, 0))],
        out_shape=[jax.ShapeDtypeStruct((t, LANES), BF16), jax.ShapeDtypeStruct((t, n), F32)],
        compiler_params=_cparams("arbitrary"),
        name="ab_small",
    )(hs, cos_t, sin_t, wgu_pad, b_gate.reshape(1, n))


def _mla_attn_kernel(q_ref, kn_ref, kpe_ref, v_ref, o_ref, m_ref, l_ref, acc_ref, *, tq, tk):
    qi = pl.program_id(2)
    ki = pl.program_id(3)

    @pl.when(ki == 0)
    def _():
        m_ref[...] = jnp.full_like(m_ref, -jnp.inf)
        l_ref[...] = jnp.zeros_like(l_ref)
        acc_ref[...] = jnp.zeros_like(acc_ref)

    @pl.when(ki <= qi)
    def _():
        k = jnp.concatenate([kn_ref[...], kpe_ref[...]], axis=1)
        s = lax.dot_general(q_ref[...], k, NT, preferred_element_type=F32)
        row = qi * tq + lax.broadcasted_iota(jnp.int32, (tq, tk), 0)
        col = ki * tk + lax.broadcasted_iota(jnp.int32, (tq, tk), 1)
        s = jnp.where(col <= row, s, -jnp.inf)
        m_prev = m_ref[...]
        m_new = jnp.maximum(m_prev, jnp.max(s, axis=-1, keepdims=True))
        p = jnp.exp(s - m_new)
        alpha = jnp.exp(m_prev - m_new)
        l_ref[...] = alpha * l_ref[...] + jnp.sum(p, axis=-1, keepdims=True)
        acc_ref[...] = alpha * acc_ref[...] + jnp.dot(p.astype(BF16), v_ref[...],
                                                      preferred_element_type=F32)
        m_ref[...] = m_new

    @pl.when(ki == qi)
    def _():
        o_ref[...] = (acc_ref[...] / l_ref[...]).astype(o_ref.dtype)


def mla_attention(q, kv, kpe, batch, heads):
    t = q.shape[0]
    s = t // batch
    tq = tk = _pick(s, (512, 256, 128))
    nq = s // tq
    return pl.pallas_call(
        functools.partial(_mla_attn_kernel, tq=tq, tk=tk),
        grid=(batch, heads, nq, nq),
        in_specs=[
            pl.BlockSpec((tq, 2 * LANES), lambda b, h, qi, ki: (b * nq + qi, h)),
            pl.BlockSpec((tk, LANES), lambda b, h, qi, ki: (b * nq + jnp.minimum(ki, qi), h)),
            pl.BlockSpec((tk, LANES), lambda b, h, qi, ki: (b * nq + jnp.minimum(ki, qi), 0)),
            pl.BlockSpec((tk, LANES), lambda b, h, qi, ki: (b * nq + jnp.minimum(ki, qi), heads + h)),
        ],
        out_specs=pl.BlockSpec((tq, LANES), lambda b, h, qi, ki: (b * nq + qi, h)),
        out_shape=jax.ShapeDtypeStruct((t, heads * LANES), BF16),
        scratch_shapes=[pltpu.VMEM((tq, 1), F32), pltpu.VMEM((tq, 1), F32),
                        pltpu.VMEM((tq, LANES), F32)],
        compiler_params=_cparams("arbitrary", "arbitrary", "arbitrary", "arbitrary"),
        name="mla_attention",
    )(q, kv, kpe, kv)


def _gla_kernel(q_ref, k_ref, v_ref, la_ref, r_ref, gn_ref, o_ref, st_ref, *, dk):
    c = pl.program_id(2)

    @pl.when(c == 0)
    def _():
        st_ref[...] = jnp.zeros_like(st_ref)

    ri = lax.broadcasted_iota(jnp.int32, (CHUNK, CHUNK), 0)
    ci = lax.broadcasted_iota(jnp.int32, (CHUNK, CHUNK), 1)
    tril = ci <= ri
    bcum = _dot_mask(tril.astype(BF16), la_ref[...])
    b_last = bcum[CHUNK - 1:CHUNK, :]
    q = q_ref[...] * (dk ** -0.5)
    k = k_ref[...]
    v = v_ref[...].astype(BF16)
    q_in = (q * jnp.exp(bcum)).astype(BF16)
    k_in = k * jnp.exp(-bcum)
    k_tail = k * jnp.exp(b_last - bcum)
    attn = jnp.where(tril, _dot(q_in, k_in, NT), 0.0)
    st = st_ref[...]
    o = _dot(attn, v) + _dot(q_in, st, NT)
    st_ref[...] = st * jnp.exp(b_last) + _dot(v, k_tail, TN)
    o_ref[...] = (_rms_norm(o, gn_ref[...]) * _silu(r_ref[...])).astype(o_ref.dtype)


def gla_mixer(h_main, log_a, gnorm, batch, heads, dk, dv, q_off, k_off, v_off, r_off):
    t = h_main.shape[0]
    n = t // batch // CHUNK
    return pl.pallas_call(
        functools.partial(_gla_kernel, dk=dk),
        grid=(batch, heads, n),
        in_specs=[
            pl.BlockSpec((CHUNK, dk), lambda b, h, c: (b * n + c, q_off // dk + h)),
            pl.BlockSpec((CHUNK, dk), lambda b, h, c: (b * n + c, k_off // dk + h)),
            pl.BlockSpec((CHUNK, dv), lambda b, h, c: (b * n + c, v_off // dv + h)),
            pl.BlockSpec((CHUNK, dk), lambda b, h, c: (b * n + c, h)),
            pl.BlockSpec((CHUNK, dv), lambda b, h, c: (b * n + c, r_off // dv + h)),
            pl.BlockSpec((1, dv), lambda b, h, c: (0, 0)),
        ],
        out_specs=pl.BlockSpec((CHUNK, dv), lambda b, h, c: (b * n + c, h)),
        out_shape=jax.ShapeDtypeStruct((t, heads * dv), BF16),
        scratch_shapes=[pltpu.VMEM((dv, dk), F32)],
        compiler_params=_cparams("arbitrary", "arbitrary", "arbitrary"),
        name="gla_mixer",
    )(h_main, h_main, h_main, log_a, h_main, gnorm.reshape(1, dv))


CONV_ROWS = 512


def _conv_kernel(x_ref, w_ref, o_ref, *, seq, conv_k, n_qk_blocks, q_blocks, q_scale, tc):
    j = pl.program_id(1)
    w = w_ref[...]
    rows = min(CONV_ROWS, seq)

    def conv_chunk(r0, first):
        cur = x_ref[pl.ds(r0, rows), :]
        if first:
            prev = jnp.zeros((8, tc), F32)
        else:
            prev = x_ref[pl.ds(r0 - 8, 8), :]
        ext = jnp.concatenate([prev, cur], axis=0)
        acc = cur * w[conv_k - 1:conv_k, :]
        for d in range(1, conv_k):
            acc = acc + pltpu.roll(ext, d, axis=0)[8:, :] * w[conv_k - 1 - d:conv_k - d, :]
        return _silu(acc)

    def l2n(y):
        parts = []
        for g in range(tc // LANES):
            yg = y[:, g * LANES:(g + 1) * LANES]
            parts.append(yg * lax.rsqrt(jnp.sum(yg * yg, axis=-1, keepdims=True) + L2_EPS))
        return jnp.concatenate(parts, axis=1) if len(parts) > 1 else parts[0]

    for ci in range(seq // rows):
        r0 = ci * rows
        y = conv_chunk(r0, ci == 0)

        @pl.when(j < q_blocks)
        def _():
            o_ref[pl.ds(r0, rows), :] = l2n(y) * q_scale

        @pl.when(jnp.logical_and(j >= q_blocks, j < n_qk_blocks))
        def _():
            o_ref[pl.ds(r0, rows), :] = l2n(y)

        @pl.when(j >= n_qk_blocks)
        def _():
            o_ref[pl.ds(r0, rows), :] = y


def gdn_conv(hc, conv_w, batch, qk_width, conv_ch, q_scale):
    t = hc.shape[0]
    seq = t // batch
    conv_k = conv_w.shape[0]
    tc = 256
    return pl.pallas_call(
        functools.partial(_conv_kernel, seq=seq, conv_k=conv_k, n_qk_blocks=2 * qk_width // tc,
                          q_blocks=qk_width // tc, q_scale=q_scale, tc=tc),
        grid=(batch, conv_ch // tc),
        in_specs=[pl.BlockSpec((seq, tc), lambda b, j: (b, j)),
                  pl.BlockSpec((conv_k, tc), lambda b, j: (0, j))],
        out_specs=pl.BlockSpec((seq, tc), lambda b, j: (b, j)),
        out_shape=jax.ShapeDtypeStruct((t, conv_ch), F32),
        compiler_params=_cparams("arbitrary", "arbitrary"),
        name="gdn_conv",
    )(hc, conv_w)


def _gdn_gates_kernel(hs_ref, negA_ref, dt_ref, o_ref, *, nh):
    hs = hs_ref[...]
    beta = jax.nn.sigmoid(hs)
    z = pltpu.roll(hs, LANES - nh, axis=1) + dt_ref[...]
    sp = jnp.maximum(z, 0.0) + jnp.log1p(jnp.exp(-jnp.abs(z)))
    g = negA_ref[...] * sp
    lane = lax.broadcasted_iota(jnp.int32, hs.shape, 1)
    o_ref[...] = jnp.where(lane < nh, beta, pltpu.roll(g, nh, axis=1))


def gdn_gates(hs, a_log, dt_bias):
    t = hs.shape[0]
    nh = a_log.shape[0]
    tm = _pick(t, (1024, 512, 256, 128))
    pad = LANES - nh
    neg_a = jnp.pad(-jnp.exp(a_log.astype(F32)), (0, pad)).reshape(1, LANES)
    dt = jnp.pad(dt_bias.astype(F32), (0, pad)).reshape(1, LANES)
    return pl.pallas_call(
        functools.partial(_gdn_gates_kernel, nh=nh),
        grid=(t // tm,),
        in_specs=[pl.BlockSpec((tm, LANES), lambda i: (i, 0)),
                  pl.BlockSpec((1, LANES), lambda i: (0, 0)),
                  pl.BlockSpec((1, LANES), lambda i: (0, 0))],
        out_specs=pl.BlockSpec((tm, LANES), lambda i: (i, 0)),
        out_shape=jax.ShapeDtypeStruct((t, LANES), F32),
        compiler_params=_cparams("arbitrary"),
        name="gdn_gates",
    )(hs, neg_a, dt)


GDN_GROUP = 8


def _gdn_kernel(q_ref, k_ref, v_ref, z_ref, gcol_ref, grow_ref, gn_ref, o_ref, st_ref,
                *, nh, dk, dv, rep):
    c = pl.program_id(2)

    @pl.when(c == 0)
    def _():
        st_ref[...] = jnp.zeros_like(st_ref)

    ri = lax.broadcasted_iota(jnp.int32, (CHUNK, CHUNK), 0)
    ci = lax.broadcasted_iota(jnp.int32, (CHUNK, CHUNK), 1)
    incl = ci <= ri
    strict = ci < ri
    gcol = gcol_ref[...]
    grow = grow_ref[0]
    gc_cols = _dot_mask(incl.astype(BF16), gcol)
    gc_rows = _dot_mask((ri <= ci).astype(BF16), grow, mask_first=False)
    gnorm = gn_ref[...]

    for hh in range(GDN_GROUP):
        kh = hh // rep
        beta = gcol[:, hh:hh + 1]
        gc_c = gc_cols[:, nh + hh:nh + hh + 1]
        gc_r = gc_rows[nh + hh:nh + hh + 1, :]
        gc_last = gc_c[CHUNK - 1:CHUNK, :]
        q = q_ref[:, kh * dk:(kh + 1) * dk]
        k = k_ref[:, kh * dk:(kh + 1) * dk]
        v = v_ref[:, hh * dv:(hh + 1) * dv]
        decay = jnp.exp(jnp.where(incl, gc_c - gc_r, -jnp.inf))
        kb = k * beta
        m = jnp.where(strict, _dot(kb, k, NT) * decay, 0.0)
        egc = jnp.exp(gc_c)
        x = jnp.concatenate([v * beta, kb * egc], axis=1)
        x = x - _dot_hilo(m, x)
        p = m
        for _ in range(5):
            p = _dot_hilo(p, p)
            x = x + _dot_hilo(p, x)
        u = x[:, :dv]
        w = x[:, dv:]
        attn = _dot(q, k, NT) * decay
        st = st_ref[hh]
        v_new = u - _dot(w, st)
        o = _dot(q * egc, st) + _dot(attn, v_new)
        k_tail = k * jnp.exp(gc_last - gc_c)
        st_ref[hh] = jnp.exp(gc_last) * st + _dot(k_tail, v_new, TN)
        zz = z_ref[:, hh * dv:(hh + 1) * dv]
        o_ref[:, hh * dv:(hh + 1) * dv] = (_rms_norm(o, gnorm) * _silu(zz)).astype(o_ref.dtype)


def gdn_mixer(qkv, hc, gates, gates_t, gnorm, batch, nh, dk, dv, rep, z_off):
    t = qkv.shape[0]
    n = t // batch // CHUNK
    ngrp = nh // GDN_GROUP
    kw = (GDN_GROUP // rep) * dk
    vw = GDN_GROUP * dv
    nkh = nh // rep
    k_blk0 = nkh * dk // kw
    v_blk0 = 2 * nkh * dk // vw
    z_blk0 = z_off // vw
    return pl.pallas_call(
        functools.partial(_gdn_kernel, nh=GDN_GROUP, dk=dk, dv=dv, rep=rep),
        grid=(batch, ngrp, n),
        in_specs=[
            pl.BlockSpec((CHUNK, kw), lambda b, g, c: (b * n + c, g)),
            pl.BlockSpec((CHUNK, kw), lambda b, g, c: (b * n + c, k_blk0 + g)),
            pl.BlockSpec((CHUNK, vw), lambda b, g, c: (b * n + c, v_blk0 + g)),
            pl.BlockSpec((CHUNK, vw), lambda b, g, c: (b * n + c, z_blk0 + g)),
            pl.BlockSpec((CHUNK, LANES), lambda b, g, c: (b * n + c, g)),
            pl.BlockSpec((1, LANES, CHUNK), lambda b, g, c: (b * n + c, g, 0)),
            pl.BlockSpec((1, dv), lambda b, g, c: (0, 0)),
        ],
        out_specs=pl.BlockSpec((CHUNK, vw), lambda b, g, c: (b * n + c, g)),
        out_shape=jax.ShapeDtypeStruct((t, nh * dv), BF16),
        scratch_shapes=[pltpu.VMEM((GDN_GROUP, dk, dv), F32)],
        compiler_params=_cparams("arbitrary", "arbitrary", "arbitrary"),
        name="gdn_mixer",
    )(qkv, qkv, qkv, hc, gates, gates_t, gnorm.reshape(1, dv))


def _xattn_kernel(q_ref, kv_ref, o_ref, *, heads, dim):
    scale = dim ** -0.5
    for h in range(heads):
        q = (q_ref[:, h * dim:(h + 1) * dim] * scale).astype(BF16)
        k = kv_ref[:, h * dim:(h + 1) * dim]
        v = kv_ref[:, (heads + h) * dim:(heads + h + 1) * dim]
        s = lax.dot_general(q, k, NT, preferred_element_type=F32)
        s = s - jnp.max(s, axis=-1, keepdims=True)
        p = jnp.exp(s)
        p = p / jnp.sum(p, axis=-1, keepdims=True)
        o_ref[:, h * dim:(h + 1) * dim] = jnp.dot(p.astype(BF16), v,
                                                  preferred_element_type=F32).astype(o_ref.dtype)


def cross_attention(q, kv, batch, heads, dim):
    t = q.shape[0]
    s = t // batch
    m = kv.shape[0] // batch
    tq = _pick(s, (512, 256, 128))
    nq = s // tq
    return pl.pallas_call(
        functools.partial(_xattn_kernel, heads=heads, dim=dim),
        grid=(batch, nq),
        in_specs=[pl.BlockSpec((tq, heads * dim), lambda b, i: (b * nq + i, 0)),
                  pl.BlockSpec((m, 2 * heads * dim), lambda b, i: (b, 0))],
        out_specs=pl.BlockSpec((tq, heads * dim), lambda b, i: (b * nq + i, 0)),
        out_shape=jax.ShapeDtypeStruct((t, heads * dim), BF16),
        compiler_params=_cparams("arbitrary", "arbitrary"),
        name="cross_attention",
    )(q, kv)


def _rope_cols(w, half):
    x1, x2 = w[:, :half], w[:, half:2 * half]
    return jnp.concatenate([x1, x2, x2, x1], axis=1)


def kernel(x, mem, positions, mem_ln_g, mem_ln_b, ln_g, ln_b, ffn_w_gate, ffn_w_up, ffn_w_down,
           ab_w_in, mla_q_norm, mla_w_uq, mla_kv_norm, mla_w_ukv, gla_w_gate_up, gla_b_gate, gla_norm,
           ab_w_out, c_w_in, gdn_conv_w, gdn_a_log, gdn_dt_bias, gdn_norm, c_w_out,
           xa_wq, xa_wkv, xa_wo):
    batch, seq, d = x.shape
    t = batch * seq
    mem_len = mem.shape[1]
    depth = ln_g.shape[0]

    q_lora = mla_q_norm.shape[1]
    kv_lora = mla_kv_norm.shape[1]
    gla_dv = gla_norm.shape[1]
    gla_dk = gla_dv // 2
    gla_qk = gla_w_gate_up.shape[2]
    gla_heads = gla_qk // gla_dk
    gate_rank = gla_w_gate_up.shape[1]
    mla_heads = (ab_w_out.shape[1] - gla_heads * gla_dv) // LANES
    nope = LANES
    rope = mla_w_uq.shape[2] // mla_heads - nope
    half = rope // 2
    assert rope == 64 and mla_w_ukv.shape[2] == mla_heads * 2 * LANES
    gdn_vh = gdn_a_log.shape[1]
    gdn_dv = gdn_norm.shape[1]
    gdn_dk = gdn_dv
    gdn_vw = gdn_vh * gdn_dv
    gdn_conv_ch = gdn_conv_w.shape[2]
    gdn_qk = (gdn_conv_ch - gdn_vw) // 2
    gdn_kh = gdn_qk // gdn_dk
    xa_dim = LANES
    xa_heads = xa_wq.shape[2] // xa_dim

    inv = 1.0 / (ROPE_THETA ** (jnp.arange(0, rope, 2, dtype=F32) / rope))
    ang = positions.astype(F32).reshape(t, 1) * inv
    cos, sin = jnp.cos(ang), jnp.sin(ang)
    zeros = jnp.zeros((t, 2 * half), F32)
    cos_t = jnp.concatenate([cos, cos, zeros], axis=1)
    sin_t = jnp.concatenate([-sin, sin, zeros], axis=1)

    memn = ln_rows(mem.reshape(batch * mem_len, d), mem_ln_g, mem_ln_b)

    xf = x.reshape(t, d)
    xb = xf.astype(BF16)

    def ffn(xf, xb, l, s):
        h = ffn_gateup(xb, ffn_w_gate, ffn_w_up, l, s)
        return down_ln(h, ffn_w_down[l, s].astype(BF16), xf, ln_g[l, 3 * s], ln_b[l, 3 * s], 0.5)

    for l in range(depth):
        i = l // 2
        xf, xb = ffn(xf, xb, l, 0)

        if l % 2 == 0:
            w_in = ab_w_in[i]
            o = 0
            segs = []
            for sz in (q_lora, kv_lora, rope, gla_qk, gla_qk, gla_heads * gla_dv, gate_rank,
                       gla_heads * gla_dv):
                segs.append(w_in[:, o:o + sz])
                o += sz
            w_cq, w_ckv, w_kpe, w_gq, w_gk, w_gv, w_glr, w_r = segs
            w_main = jnp.concatenate([w_cq, w_ckv, w_gq, w_gk, w_gv, w_r], axis=1).astype(BF16)
            w_small = jnp.concatenate(
                [_rope_cols(w_kpe, half), w_glr, jnp.zeros((d, LANES - gate_rank), F32)],
                axis=1).astype(BF16)
            h_main = proj(xb, w_main)
            h_small = proj(xb, w_small)
            q_off = q_lora + kv_lora
            k_off = q_off + gla_qk
            v_off = k_off + gla_qk
            r_off = v_off + gla_heads * gla_dv

            wgu_pad = jnp.pad(gla_w_gate_up[i], ((0, LANES - gate_rank), (0, 0))).astype(BF16)
            kpe, log_a = ab_small(h_small, cos_t, sin_t, wgu_pad, gla_b_gate[i])

            wq = mla_w_uq[i].reshape(q_lora, mla_heads, nope + rope)
            wq = jnp.concatenate(
                [wq[:, :, :nope], wq[:, :, nope:nope + half], wq[:, :, nope + half:],
                 wq[:, :, nope + half:], wq[:, :, nope:nope + half]], axis=2)
            wq = wq.reshape(q_lora, mla_heads * 2 * LANES).astype(BF16)
            q = mla_q_proj(h_main, 0, q_lora, mla_q_norm[i], wq, cos_t, sin_t,
                           float((nope + rope) ** -0.5))
            wkv = mla_w_ukv[i].reshape(kv_lora, mla_heads, 2, LANES)
            wkv = jnp.transpose(wkv, (0, 2, 1, 3)).reshape(kv_lora, 2 * mla_heads * LANES).astype(BF16)
            kv = mla_kv_proj(h_main, q_lora // kv_lora, kv_lora, mla_kv_norm[i], wkv)
            o_mla = mla_attention(q, kv, kpe, batch, mla_heads)

            o_gla = gla_mixer(h_main, log_a, gla_norm[i], batch, gla_heads, gla_dk, gla_dv,
                              q_off, k_off, v_off, r_off)
            y_in = jnp.concatenate([o_mla, o_gla], axis=1)
            w_out = ab_w_out[i].astype(BF16)
        else:
            w_in = c_w_in[i]
            main_w = gdn_conv_ch + gdn_vw
            hc = proj(xb, w_in[:, :main_w].astype(BF16))
            w_small = jnp.pad(w_in[:, main_w:], ((0, 0), (0, LANES - 2 * gdn_vh))).astype(BF16)
            h_small = proj(xb, w_small)
            qkv = gdn_conv(hc, gdn_conv_w[i], batch, gdn_qk, gdn_conv_ch, float(gdn_dk ** -0.5))
            gates = gdn_gates(h_small, gdn_a_log[i], gdn_dt_bias[i])
            ngrp = gdn_vh // GDN_GROUP
            beta_g = gates[:, :gdn_vh].reshape(t, ngrp, GDN_GROUP)
            g_g = gates[:, gdn_vh:2 * gdn_vh].reshape(t, ngrp, GDN_GROUP)
            gg = jnp.concatenate(
                [beta_g, g_g, jnp.zeros((t, ngrp, LANES - 2 * GDN_GROUP), F32)], axis=2)
            gates_c = gg.reshape(t, ngrp * LANES)
            gates_t = jnp.transpose(gg.reshape(t // CHUNK, CHUNK, ngrp * LANES), (0, 2, 1))
            y_in = gdn_mixer(qkv, hc, gates_c, gates_t, gdn_norm[i], batch, gdn_vh, gdn_dk, gdn_dv,
                             gdn_vh // gdn_kh, gdn_conv_ch)
            w_out = c_w_out[i].astype(BF16)

        xf, xb = down_ln(y_in, w_out, xf, ln_g[l, 1], ln_b[l, 1], 1.0)

        qx = proj(xb, xa_wq[l].astype(BF16))
        kvx = proj(memn, xa_wkv[l].astype(BF16), out_dtype=BF16)
        ox = cross_attention(qx, kvx, batch, xa_heads, xa_dim)
        xf, xb = down_ln(ox, xa_wo[l].astype(BF16), xf, ln_g[l, 2], ln_b[l, 2], 1.0)

        xf, xb = ffn(xf, xb, l, 1)

    return xf.reshape(batch, seq, d)
```

```python
import functools

import jax
import jax.numpy as jnp
from jax import lax
from jax.experimental import pallas as pl
from jax.experimental.pallas import tpu as pltpu

F32 = jnp.float32
BF16 = jnp.bfloat16

VMEM_LIMIT_BYTES = 56 * 1024 * 1024
LANES = 128

DEPTH = 2
DN_ALPHA = (2 * DEPTH) ** 0.25
ROPE_THETA = 10000.0
CHUNK = 64
GLA_GATE_NORM = 16.0
LN_EPS = 1e-5
RMS_EPS = 1e-6
L2_EPS = 1e-6
LN_ROWS = 64
FFN_K_TILE = 512

NT = (((1,), (1,)), ((), ()))
TN = (((0,), (0,)), ((), ()))


def _cparams(*sem):
    return pltpu.CompilerParams(dimension_semantics=sem, vmem_limit_bytes=VMEM_LIMIT_BYTES)


def _dot(a, b, dims=None):
    a = a.astype(BF16)
    b = b.astype(BF16)
    if dims is None:
        return jnp.dot(a, b, preferred_element_type=F32)
    return lax.dot_general(a, b, dims, preferred_element_type=F32)


def _split3(x):
    hi = x.astype(BF16)
    r = x - hi.astype(F32)
    mid = r.astype(BF16)
    lo = (r - mid.astype(F32)).astype(BF16)
    return hi, mid, lo


def _dot_mask(mask_bf16, x, dims=None, mask_first=True):
    out = None
    for piece in _split3(x):
        if mask_first:
            p = (jnp.dot(mask_bf16, piece, preferred_element_type=F32) if dims is None
                 else lax.dot_general(mask_bf16, piece, dims, preferred_element_type=F32))
        else:
            p = (jnp.dot(piece, mask_bf16, preferred_element_type=F32) if dims is None
                 else lax.dot_general(piece, mask_bf16, dims, preferred_element_type=F32))
        out = p if out is None else out + p
    return out


def _dot_hilo(a, b):
    a_hi = a.astype(BF16)
    a_lo = (a - a_hi.astype(F32)).astype(BF16)
    b_hi = b.astype(BF16)
    b_lo = (b - b_hi.astype(F32)).astype(BF16)
    return (jnp.dot(a_hi, b_hi, preferred_element_type=F32)
            + jnp.dot(a_hi, b_lo, preferred_element_type=F32)
            + jnp.dot(a_lo, b_hi, preferred_element_type=F32))


def _silu(x):
    return x * jax.nn.sigmoid(x)


def _layer_norm(y, g, b):
    mu = jnp.mean(y, axis=-1, keepdims=True)
    yc = y - mu
    var = jnp.mean(yc * yc, axis=-1, keepdims=True)
    return yc * lax.rsqrt(var + LN_EPS) * g + b


def _rms_norm(y, g):
    return y * lax.rsqrt(jnp.mean(y * y, axis=-1, keepdims=True) + RMS_EPS) * g


def _pick(n, prefs):
    for p in prefs:
        if n % p == 0:
            return p
    return n


def _ln_rows_kernel(x_ref, g_ref, b_ref, ob_ref):
    ob_ref[...] = _layer_norm(x_ref[...], g_ref[...], b_ref[...]).astype(ob_ref.dtype)


def ln_rows(x, g, b):
    m, d = x.shape
    tm = _pick(m, (256, 128, 64, 32, 16))
    return pl.pallas_call(
        _ln_rows_kernel,
        grid=(m // tm,),
        in_specs=[pl.BlockSpec((tm, d), lambda i: (i, 0)),
                  pl.BlockSpec((1, d), lambda i: (0, 0)),
                  pl.BlockSpec((1, d), lambda i: (0, 0))],
        out_specs=pl.BlockSpec((tm, d), lambda i: (i, 0)),
        out_shape=jax.ShapeDtypeStruct((m, d), BF16),
        compiler_params=_cparams("arbitrary"),
        name="ln_rows",
    )(x, g.reshape(1, d), b.reshape(1, d))


def _gateup_kernel(x_ref, wg_ref, wu_ref, h_ref, wgb_ref, wub_ref, *, n_real):
    j = pl.program_id(0)

    @pl.when(jnp.logical_and(pl.program_id(1) == 0, j < n_real))
    def _():
        wgb_ref[...] = wg_ref[...].astype(BF16)
        wub_ref[...] = wu_ref[...].astype(BF16)

    @pl.when(j < n_real)
    def _():
        x = x_ref[...]
        g = jnp.dot(x, wgb_ref[...], preferred_element_type=F32)
        u = jnp.dot(x, wub_ref[...], preferred_element_type=F32)
        h_ref[...] = (_silu(g) * u).astype(h_ref.dtype)

    @pl.when(j >= n_real)
    def _():
        h_ref[...] = jnp.zeros_like(h_ref)


def ffn_gateup(xb, wg_all, wu_all, l, s, f_pad):
    t, d = xb.shape
    f = wg_all.shape[-1]
    tm = _pick(t, (1024, 512, 256, 128))
    tn = _pick(f, (256, 128))
    n_real = f // tn
    wspec = pl.BlockSpec((None, None, d, tn), lambda j, i: (l, s, 0, jnp.minimum(j, n_real - 1)))
    return pl.pallas_call(
        functools.partial(_gateup_kernel, n_real=n_real),
        grid=(f_pad // tn, t // tm),
        in_specs=[pl.BlockSpec((tm, d), lambda j, i: (i, 0)), wspec, wspec],
        out_specs=pl.BlockSpec((tm, tn), lambda j, i: (i, j)),
        out_shape=jax.ShapeDtypeStruct((t, f_pad), BF16),
        scratch_shapes=[pltpu.VMEM((d, tn), BF16), pltpu.VMEM((d, tn), BF16)],
        compiler_params=_cparams("arbitrary", "arbitrary"),
        name="ffn_gateup",
    )(xb, wg_all, wu_all)


def _down_ln_kernel(a_ref, w_ref, r_ref, g_ref, b_ref, o_ref, ob_ref, *, scale, nk):
    k = pl.program_id(1)

    @pl.when(k == 0)
    def _():
        o_ref[...] = jnp.zeros_like(o_ref)

    o_ref[...] += jnp.dot(a_ref[...], w_ref[...], preferred_element_type=F32)

    @pl.when(k == nk - 1)
    def _():
        g = g_ref[...]
        b = b_ref[...]
        rc = min(LN_ROWS, o_ref.shape[0])

        def body(r, carry):
            rows = pl.ds(pl.multiple_of(r * rc, rc), rc)
            y = DN_ALPHA * r_ref[rows, :] + scale * o_ref[rows, :]
            out = _layer_norm(y, g, b)
            o_ref[rows, :] = out
            ob_ref[rows, :] = out.astype(ob_ref.dtype)
            return carry

        lax.fori_loop(0, o_ref.shape[0] // rc, body, 0)


def down_ln(a, w, resid, g, b, scale):
    t, kdim = a.shape
    d = w.shape[1]
    tm = _pick(t, (512, 256, 128))
    tk = _pick(kdim, (512, 256, 128))
    nk = kdim // tk
    return pl.pallas_call(
        functools.partial(_down_ln_kernel, scale=scale, nk=nk),
        grid=(t // tm, nk),
        in_specs=[pl.BlockSpec((tm, tk), lambda i, k: (i, k)),
                  pl.BlockSpec((tk, d), lambda i, k: (k, 0)),
                  pl.BlockSpec((tm, d), lambda i, k: (i, 0), pipeline_mode=pl.Buffered(1)),
                  pl.BlockSpec((1, d), lambda i, k: (0, 0)),
                  pl.BlockSpec((1, d), lambda i, k: (0, 0))],
        out_specs=[pl.BlockSpec((tm, d), lambda i, k: (i, 0)),
                   pl.BlockSpec((tm, d), lambda i, k: (i, 0))],
        out_shape=[jax.ShapeDtypeStruct((t, d), F32), jax.ShapeDtypeStruct((t, d), BF16)],
        compiler_params=_cparams("arbitrary", "arbitrary"),
        name="down_ln",
    )(a, w, resid, g.reshape(1, d), b.reshape(1, d))


def _proj_kernel(a_ref, w_ref, o_ref):
    o_ref[...] = jnp.dot(a_ref[...], w_ref[...], preferred_element_type=F32).astype(o_ref.dtype)


def proj(a, w, out_dtype=F32):
    t, kdim = a.shape
    n = w.shape[1]
    tm = _pick(t, (1024, 512, 256, 128))
    tn = _pick(n, (1024, 768, 512, 384, 256, 128))
    return pl.pallas_call(
        _proj_kernel,
        grid=(t // tm, n // tn),
        in_specs=[pl.BlockSpec((tm, kdim), lambda i, j: (i, 0)),
                  pl.BlockSpec((kdim, tn), lambda i, j: (0, j))],
        out_specs=pl.BlockSpec((tm, tn), lambda i, j: (i, j)),
        out_shape=jax.ShapeDtypeStruct((t, n), out_dtype),
        compiler_params=_cparams("arbitrary", "arbitrary"),
        name="proj",
    )(a, w)


def _rope128(v, c, s):
    return v * c + pltpu.roll(v, 64, axis=1) * s


def _mla_q_kernel(c_ref, gn_ref, w_ref, cos_ref, sin_ref, q_ref, *, heads_per_tile, scale):
    n = _rms_norm(c_ref[...], gn_ref[...]).astype(BF16)
    q = jnp.dot(n, w_ref[...], preferred_element_type=F32) * scale
    c = cos_ref[...]
    s = sin_ref[...]
    for h in range(heads_per_tile):
        o = h * 2 * LANES
        q_ref[:, o:o + LANES] = q[:, o:o + LANES].astype(q_ref.dtype)
        q_ref[:, o + LANES:o + 2 * LANES] = _rope128(q[:, o + LANES:o + 2 * LANES], c, s).astype(q_ref.dtype)


def mla_q_proj(h_main, col_block, rank, gnorm, w, cos_t, sin_t, scale):
    t = h_main.shape[0]
    n = w.shape[1]
    tm = _pick(t, (512, 256, 128))
    tn = _pick(n, (1024, 512, 256))
    return pl.pallas_call(
        functools.partial(_mla_q_kernel, heads_per_tile=tn // (2 * LANES), scale=scale),
        grid=(t // tm, n // tn),
        in_specs=[pl.BlockSpec((tm, rank), lambda i, j: (i, col_block)),
                  pl.BlockSpec((1, rank), lambda i, j: (0, 0)),
                  pl.BlockSpec((rank, tn), lambda i, j: (0, j)),
                  pl.BlockSpec((tm, LANES), lambda i, j: (i, 0)),
                  pl.BlockSpec((tm, LANES), lambda i, j: (i, 0))],
        out_specs=pl.BlockSpec((tm, tn), lambda i, j: (i, j)),
        out_shape=jax.ShapeDtypeStruct((t, n), BF16),
        compiler_params=_cparams("arbitrary", "arbitrary"),
        name="mla_q_proj",
    )(h_main, gnorm.reshape(1, rank), w, cos_t, sin_t)


def _mla_kv_kernel(c_ref, gn_ref, w_ref, o_ref):
    n = _rms_norm(c_ref[...], gn_ref[...]).astype(BF16)
    o_ref[...] = jnp.dot(n, w_ref[...], preferred_element_type=F32).astype(o_ref.dtype)


def mla_kv_proj(h_main, col_block, rank, gnorm, w):
    t = h_main.shape[0]
    n = w.shape[1]
    tm = _pick(t, (512, 256, 128))
    tn = _pick(n, (1024, 512, 256, 128))
    return pl.pallas_call(
        _mla_kv_kernel,
        grid=(t // tm, n // tn),
        in_specs=[pl.BlockSpec((tm, rank), lambda i, j: (i, col_block)),
                  pl.BlockSpec((1, rank), lambda i, j: (0, 0)),
                  pl.BlockSpec((rank, tn), lambda i, j: (0, j))],
        out_specs=pl.BlockSpec((tm, tn), lambda i, j: (i, j)),
        out_shape=jax.ShapeDtypeStruct((t, n), BF16),
        compiler_params=_cparams("arbitrary", "arbitrary"),
        name="mla_kv_proj",
    )(h_main, gnorm.reshape(1, rank), w)


def _ab_small_kernel(hs_ref, cos_ref, sin_ref, wgu_ref, bg_ref, kpe_ref, la_ref):
    hs = hs_ref[...]
    kpe_ref[...] = _rope128(hs[:, :LANES], cos_ref[...], sin_ref[...]).astype(kpe_ref.dtype)
    z = jnp.dot(hs[:, LANES:].astype(BF16), wgu_ref[...], preferred_element_type=F32) + bg_ref[...]
    log_sig = jnp.minimum(z, 0.0) - jnp.log1p(jnp.exp(-jnp.abs(z)))
    la_ref[...] = log_sig / GLA_GATE_NORM


def ab_small(hs, cos_t, sin_t, wgu_pad, b_gate):
    t = hs.shape[0]
    n = wgu_pad.shape[1]
    tm = _pick(t, (512, 256, 128))
    return pl.pallas_call(
        _ab_small_kernel,
        grid=(t // tm,),
        in_specs=[pl.BlockSpec((tm, 2 * LANES), lambda i: (i, 0)),
                  pl.BlockSpec((tm, LANES), lambda i: (i, 0)),
                  pl.BlockSpec((tm, LANES), lambda i: (i, 0)),
                  pl.BlockSpec((LANES, n), lambda i: (0, 0)),
                  pl.BlockSpec((1, n), lambda i: (0, 0))],
        out_specs=[pl.BlockSpec((tm, LANES), lambda i: (i, 0)),
                   pl.BlockSpec((tm, n), lambda i: (i, 0))],
        out_shape=[jax.ShapeDtypeStruct((t, LANES), BF16), jax.ShapeDtypeStruct((t, n), F32)],
        compiler_params=_cparams("arbitrary"),
        name="ab_small",
    )(hs, cos_t, sin_t, wgu_pad, b_gate.reshape(1, n))


def _mla_attn_kernel(q_ref, kn_ref, kpe_ref, v_ref, o_ref, m_ref, l_ref, acc_ref, *, tq):
    qi = pl.program_id(2)
    q = q_ref[...]
    m_ref[...] = jnp.full_like(m_ref, -jnp.inf)
    l_ref[...] = jnp.zeros_like(l_ref)
    acc_ref[...] = jnp.zeros_like(acc_ref)

    def step(ki, masked):
        rows = pl.ds(pl.multiple_of(ki * tq, tq), tq)
        k = jnp.concatenate([kn_ref[rows, :], kpe_ref[rows, :]], axis=1)
        s = lax.dot_general(q, k, NT, preferred_element_type=F32)
        if masked:
            row = lax.broadcasted_iota(jnp.int32, (tq, tq), 0)
            col = lax.broadcasted_iota(jnp.int32, (tq, tq), 1)
            s = jnp.where(col <= row, s, -jnp.inf)
        m_prev = m_ref[...]
        m_new = jnp.maximum(m_prev, jnp.max(s, axis=-1, keepdims=True))
        p = jnp.exp(s - m_new)
        alpha = jnp.exp(m_prev - m_new)
        l_ref[...] = alpha * l_ref[...] + jnp.sum(p, axis=-1, keepdims=True)
        acc_ref[...] = alpha * acc_ref[...] + jnp.dot(p.astype(BF16), v_ref[rows, :],
                                                      preferred_element_type=F32)
        m_ref[...] = m_new

    def body(ki, carry):
        step(ki, False)
        return carry

    lax.fori_loop(0, qi, body, 0)
    step(qi, True)
    o_ref[...] = (acc_ref[...] / l_ref[...]).astype(o_ref.dtype)


def mla_attention(q, kv, kpe, batch, heads):
    t = q.shape[0]
    s = t // batch
    tq = _pick(s, (512, 256, 128))
    nq = s // tq
    return pl.pallas_call(
        functools.partial(_mla_attn_kernel, tq=tq),
        grid=(batch, heads, nq),
        in_specs=[
            pl.BlockSpec((tq, 2 * LANES), lambda b, h, qi: (b * nq + qi, h)),
            pl.BlockSpec((s, LANES), lambda b, h, qi: (b, h)),
            pl.BlockSpec((s, LANES), lambda b, h, qi: (b, 0)),
            pl.BlockSpec((s, LANES), lambda b, h, qi: (b, heads + h)),
        ],
        out_specs=pl.BlockSpec((tq, LANES), lambda b, h, qi: (b * nq + qi, h)),
        out_shape=jax.ShapeDtypeStruct((t, heads * LANES), BF16),
        scratch_shapes=[pltpu.VMEM((tq, 1), F32), pltpu.VMEM((tq, 1), F32),
                        pltpu.VMEM((tq, LANES), F32)],
        compiler_params=_cparams("arbitrary", "arbitrary", "arbitrary"),
        name="mla_attention",
    )(q, kv, kpe, kv)


def _gla_kernel(q_ref, k_ref, v_ref, la_ref, r_ref, gn_ref, o_ref, st_ref, *, dk):
    c = pl.program_id(2)

    @pl.when(c == 0)
    def _():
        st_ref[...] = jnp.zeros_like(st_ref)

    ri = lax.broadcasted_iota(jnp.int32, (CHUNK, CHUNK), 0)
    ci = lax.broadcasted_iota(jnp.int32, (CHUNK, CHUNK), 1)
    tril = ci <= ri
    bcum = _dot_mask(tril.astype(BF16), la_ref[...])
    b_last = bcum[CHUNK - 1:CHUNK, :]
    q = q_ref[...] * (dk ** -0.5)
    k = k_ref[...]
    v = v_ref[...].astype(BF16)
    q_in = (q * jnp.exp(bcum)).astype(BF16)
    k_in = k * jnp.exp(-bcum)
    k_tail = k * jnp.exp(b_last - bcum)
    attn = jnp.where(tril, _dot(q_in, k_in, NT), 0.0)
    st = st_ref[...]
    o = _dot(attn, v) + _dot(q_in, st, NT)
    st_ref[...] = st * jnp.exp(b_last) + _dot(v, k_tail, TN)
    o_ref[...] = (_rms_norm(o, gn_ref[...]) * _silu(r_ref[...])).astype(o_ref.dtype)


def gla_mixer(h_main, log_a, gnorm, batch, heads, dk, dv, q_off, k_off, v_off, r_off):
    t = h_main.shape[0]
    n = t // batch // CHUNK
    return pl.pallas_call(
        functools.partial(_gla_kernel, dk=dk),
        grid=(batch, heads, n),
        in_specs=[
            pl.BlockSpec((CHUNK, dk), lambda b, h, c: (b * n + c, q_off // dk + h)),
            pl.BlockSpec((CHUNK, dk), lambda b, h, c: (b * n + c, k_off // dk + h)),
            pl.BlockSpec((CHUNK, dv), lambda b, h, c: (b * n + c, v_off // dv + h)),
            pl.BlockSpec((CHUNK, dk), lambda b, h, c: (b * n + c, h)),
            pl.BlockSpec((CHUNK, dv), lambda b, h, c: (b * n + c, r_off // dv + h)),
            pl.BlockSpec((1, dv), lambda b, h, c: (0, 0)),
        ],
        out_specs=pl.BlockSpec((CHUNK, dv), lambda b, h, c: (b * n + c, h)),
        out_shape=jax.ShapeDtypeStruct((t, heads * dv), BF16),
        scratch_shapes=[pltpu.VMEM((dv, dk), F32)],
        compiler_params=_cparams("arbitrary", "arbitrary", "arbitrary"),
        name="gla_mixer",
    )(h_main, h_main, h_main, log_a, h_main, gnorm.reshape(1, dv))


CONV_ROWS = 512


def _conv_kernel(x_ref, w_ref, o_ref, *, seq, conv_k, n_qk_blocks, q_blocks, q_scale, tc):
    j = pl.program_id(1)
    w = w_ref[...]
    rows = min(CONV_ROWS, seq)

    def conv_chunk(r0, first):
        cur = x_ref[pl.ds(r0, rows), :]
        if first:
            prev = jnp.zeros((8, tc), F32)
        else:
            prev = x_ref[pl.ds(r0 - 8, 8), :]
        ext = jnp.concatenate([prev, cur], axis=0)
        acc = cur * w[conv_k - 1:conv_k, :]
        for d in range(1, conv_k):
            acc = acc + pltpu.roll(ext, d, axis=0)[8:, :] * w[conv_k - 1 - d:conv_k - d, :]
        return _silu(acc)

    def l2n(y):
        parts = []
        for g in range(tc // LANES):
            yg = y[:, g * LANES:(g + 1) * LANES]
            parts.append(yg * lax.rsqrt(jnp.sum(yg * yg, axis=-1, keepdims=True) + L2_EPS))
        return jnp.concatenate(parts, axis=1) if len(parts) > 1 else parts[0]

    for ci in range(seq // rows):
        r0 = ci * rows
        y = conv_chunk(r0, ci == 0)

        @pl.when(j < q_blocks)
        def _():
            o_ref[pl.ds(r0, rows), :] = l2n(y) * q_scale

        @pl.when(jnp.logical_and(j >= q_blocks, j < n_qk_blocks))
        def _():
            o_ref[pl.ds(r0, rows), :] = l2n(y)

        @pl.when(j >= n_qk_blocks)
        def _():
            o_ref[pl.ds(r0, rows), :] = y


def gdn_conv(hc, conv_w, batch, qk_width, conv_ch, q_scale):
    t = hc.shape[0]
    seq = t // batch
    conv_k = conv_w.shape[0]
    tc = 256
    return pl.pallas_call(
        functools.partial(_conv_kernel, seq=seq, conv_k=conv_k, n_qk_blocks=2 * qk_width // tc,
                          q_blocks=qk_width // tc, q_scale=q_scale, tc=tc),
        grid=(batch, conv_ch // tc),
        in_specs=[pl.BlockSpec((seq, tc), lambda b, j: (b, j)),
                  pl.BlockSpec((conv_k, tc), lambda b, j: (0, j))],
        out_specs=pl.BlockSpec((seq, tc), lambda b, j: (b, j)),
        out_shape=jax.ShapeDtypeStruct((t, conv_ch), F32),
        compiler_params=_cparams("arbitrary", "arbitrary"),
        name="gdn_conv",
    )(hc, conv_w)


def _gdn_gates_kernel(hs_ref, negA_ref, dt_ref, o_ref, *, nh):
    hs = hs_ref[...]
    beta = jax.nn.sigmoid(hs)
    z = pltpu.roll(hs, LANES - nh, axis=1) + dt_ref[...]
    sp = jnp.maximum(z, 0.0) + jnp.log1p(jnp.exp(-jnp.abs(z)))
    g = negA_ref[...] * sp
    lane = lax.broadcasted_iota(jnp.int32, hs.shape, 1)
    o_ref[...] = jnp.where(lane < nh, beta, pltpu.roll(g, nh, axis=1))


def gdn_gates(hs, a_log, dt_bias):
    t = hs.shape[0]
    nh = a_log.shape[0]
    tm = _pick(t, (1024, 512, 256, 128))
    pad = LANES - nh
    neg_a = jnp.pad(-jnp.exp(a_log.astype(F32)), (0, pad)).reshape(1, LANES)
    dt = jnp.pad(dt_bias.astype(F32), (0, pad)).reshape(1, LANES)
    return pl.pallas_call(
        functools.partial(_gdn_gates_kernel, nh=nh),
        grid=(t // tm,),
        in_specs=[pl.BlockSpec((tm, LANES), lambda i: (i, 0)),
                  pl.BlockSpec((1, LANES), lambda i: (0, 0)),
                  pl.BlockSpec((1, LANES), lambda i: (0, 0))],
        out_specs=pl.BlockSpec((tm, LANES), lambda i: (i, 0)),
        out_shape=jax.ShapeDtypeStruct((t, LANES), F32),
        compiler_params=_cparams("arbitrary"),
        name="gdn_gates",
    )(hs, neg_a, dt)


GDN_GROUP = 8
INV_BLOCK_SHIFTS = (3, 4, 5)


def _unit_lower_inverses(ms, same_blk, eye):
    m0 = [jnp.where(same_blk[0], m, 0.0) for m in ms]
    m2 = [_dot(x, x) for x in m0]
    a = [eye - x for x in m0]
    t = [_dot(x, y) for x, y in zip(a, m2)]
    m4 = [_dot(x, x) for x in m2]
    a = [x + y for x, y in zip(a, t)]
    t = [_dot(x, y) for x, y in zip(a, m4)]
    a = [x + y for x, y in zip(a, t)]
    inner = same_blk[0]
    for outer in same_blk[1:] + [None]:
        if outer is None:
            sel = jnp.logical_not(inner)
        else:
            sel = jnp.logical_and(outer, jnp.logical_not(inner))
            inner = outer
        ea = [_dot(jnp.where(sel, m, 0.0), x) for m, x in zip(ms, a)]
        t = [_dot(x, y) for x, y in zip(a, ea)]
        a = [x - y for x, y in zip(a, t)]
    return a


def _gdn_kernel(q_ref, k_ref, v_ref, z_ref, gcol_ref, grow_ref, gn_ref, o_ref, st_ref,
                *, nh, dk, dv, rep):
    c = pl.program_id(2)

    @pl.when(c == 0)
    def _():
        st_ref[...] = jnp.zeros_like(st_ref)

    ri = lax.broadcasted_iota(jnp.int32, (CHUNK, CHUNK), 0)
    ci = lax.broadcasted_iota(jnp.int32, (CHUNK, CHUNK), 1)
    incl = ci <= ri
    strict = ci < ri
    gcol = gcol_ref[...]
    grow = grow_ref[0]
    gc_cols = _dot_mask(incl.astype(BF16), gcol)
    gc_rows = _dot_mask((ri <= ci).astype(BF16), grow, mask_first=False)
    gnorm = gn_ref[...]
    same_blk = [(ri >> sh) == (ci >> sh) for sh in INV_BLOCK_SHIFTS]
    eye = (ri == ci).astype(F32)

    heads = range(nh)

    ks = [k_ref[:, kh * dk:(kh + 1) * dk] for kh in range(nh // rep)]
    qs = [q_ref[:, kh * dk:(kh + 1) * dk] for kh in range(nh // rep)]
    kk = [_dot(k, k, NT) for k in ks]
    qk = [_dot(q, k, NT) for q, k in zip(qs, ks)]

    beta = [gcol[:, hh:hh + 1] for hh in heads]
    gc_c = [gc_cols[:, nh + hh:nh + hh + 1] for hh in heads]
    gc_r = [gc_rows[nh + hh:nh + hh + 1, :] for hh in heads]
    gc_last = [g[CHUNK - 1:CHUNK, :] for g in gc_c]
    egc = [jnp.exp(g) for g in gc_c]
    decay = [jnp.exp(jnp.where(incl, gc_c[hh] - gc_r[hh], -jnp.inf)) for hh in heads]
    ms = [jnp.where(strict, beta[hh] * kk[hh // rep] * decay[hh], 0.0) for hh in heads]
    rhs = [jnp.concatenate([v_ref[:, hh * dv:(hh + 1) * dv] * beta[hh],
                            ks[hh // rep] * (beta[hh] * egc[hh])], axis=1) for hh in heads]
    inv = _unit_lower_inverses(ms, same_blk, eye)
    x = [_dot(a, r) for a, r in zip(inv, rhs)]
    st = [st_ref[hh] for hh in heads]
    ws = [_dot(x[hh][:, dv:], st[hh]) for hh in heads]
    o_st = [_dot(qs[hh // rep] * egc[hh], st[hh]) for hh in heads]
    v_new = [x[hh][:, :dv] - ws[hh] for hh in heads]
    o_in = [_dot(qk[hh // rep] * decay[hh], v_new[hh]) for hh in heads]
    kv = [_dot(ks[hh // rep] * jnp.exp(gc_last[hh] - gc_c[hh]), v_new[hh], TN) for hh in heads]
    for hh in heads:
        st_ref[hh] = jnp.exp(gc_last[hh]) * st[hh] + kv[hh]
        zz = z_ref[:, hh * dv:(hh + 1) * dv]
        o = o_st[hh] + o_in[hh]
        o_ref[:, hh * dv:(hh + 1) * dv] = (_rms_norm(o, gnorm) * _silu(zz)).astype(o_ref.dtype)


def gdn_mixer(qkv, hc, gates, gates_t, gnorm, batch, nh, dk, dv, rep, z_off):
    t = qkv.shape[0]
    n = t // batch // CHUNK
    ngrp = nh // GDN_GROUP
    kw = (GDN_GROUP // rep) * dk
    vw = GDN_GROUP * dv
    nkh = nh // rep
    k_blk0 = nkh * dk // kw
    v_blk0 = 2 * nkh * dk // vw
    z_blk0 = z_off // vw
    return pl.pallas_call(
        functools.partial(_gdn_kernel, nh=GDN_GROUP, dk=dk, dv=dv, rep=rep),
        grid=(batch, ngrp, n),
        in_specs=[
            pl.BlockSpec((CHUNK, kw), lambda b, g, c: (b * n + c, g)),
            pl.BlockSpec((CHUNK, kw), lambda b, g, c: (b * n + c, k_blk0 + g)),
            pl.BlockSpec((CHUNK, vw), lambda b, g, c: (b * n + c, v_blk0 + g)),
            pl.BlockSpec((CHUNK, vw), lambda b, g, c: (b * n + c, z_blk0 + g)),
            pl.BlockSpec((CHUNK, LANES), lambda b, g, c: (b * n + c, g)),
            pl.BlockSpec((1, LANES, CHUNK), lambda b, g, c: (b * n + c, g, 0)),
            pl.BlockSpec((1, dv), lambda b, g, c: (0, 0)),
        ],
        out_specs=pl.BlockSpec((CHUNK, vw), lambda b, g, c: (b * n + c, g)),
        out_shape=jax.ShapeDtypeStruct((t, nh * dv), BF16),
        scratch_shapes=[pltpu.VMEM((GDN_GROUP, dk, dv), F32)],
        compiler_params=_cparams("arbitrary", "arbitrary", "arbitrary"),
        name="gdn_mixer",
    )(qkv, qkv, qkv, hc, gates, gates_t, gnorm.reshape(1, dv))


def _xattn_kernel(q_ref, kv_ref, o_ref, *, heads, dim):
    scale = dim ** -0.5
    for h in range(heads):
        q = (q_ref[:, h * dim:(h + 1) * dim] * scale).astype(BF16)
        k = kv_ref[:, h * dim:(h + 1) * dim]
        v = kv_ref[:, (heads + h) * dim:(heads + h + 1) * dim]
        s = lax.dot_general(q, k, NT, preferred_element_type=F32)
        s = s - jnp.max(s, axis=-1, keepdims=True)
        p = jnp.exp(s)
        p = p / jnp.sum(p, axis=-1, keepdims=True)
        o_ref[:, h * dim:(h + 1) * dim] = jnp.dot(p.astype(BF16), v,
                                                  preferred_element_type=F32).astype(o_ref.dtype)


def cross_attention(q, kv, batch, heads, dim):
    t = q.shape[0]
    s = t // batch
    m = kv.shape[0] // batch
    tq = _pick(s, (512, 256, 128))
    nq = s // tq
    return pl.pallas_call(
        functools.partial(_xattn_kernel, heads=heads, dim=dim),
        grid=(batch, nq),
        in_specs=[pl.BlockSpec((tq, heads * dim), lambda b, i: (b * nq + i, 0)),
                  pl.BlockSpec((m, 2 * heads * dim), lambda b, i: (b, 0))],
        out_specs=pl.BlockSpec((tq, heads * dim), lambda b, i: (b * nq + i, 0)),
        out_shape=jax.ShapeDtypeStruct((t, heads * dim), BF16),
        compiler_params=_cparams("arbitrary", "arbitrary"),
        name="cross_attention",
    )(q, kv)


def _rope_cols(w, half):
    x1, x2 = w[:, :half], w[:, half:2 * half]
    return jnp.concatenate([x1, x2, x2, x1], axis=1)


def kernel(x, mem, positions, mem_ln_g, mem_ln_b, ln_g, ln_b, ffn_w_gate, ffn_w_up, ffn_w_down,
           ab_w_in, mla_q_norm, mla_w_uq, mla_kv_norm, mla_w_ukv, gla_w_gate_up, gla_b_gate, gla_norm,
           ab_w_out, c_w_in, gdn_conv_w, gdn_a_log, gdn_dt_bias, gdn_norm, c_w_out,
           xa_wq, xa_wkv, xa_wo):
    batch, seq, d = x.shape
    t = batch * seq
    mem_len = mem.shape[1]
    depth = ln_g.shape[0]

    q_lora = mla_q_norm.shape[1]
    kv_lora = mla_kv_norm.shape[1]
    gla_dv = gla_norm.shape[1]
    gla_dk = gla_dv // 2
    gla_qk = gla_w_gate_up.shape[2]
    gla_heads = gla_qk // gla_dk
    gate_rank = gla_w_gate_up.shape[1]
    mla_heads = (ab_w_out.shape[1] - gla_heads * gla_dv) // LANES
    nope = LANES
    rope = mla_w_uq.shape[2] // mla_heads - nope
    half = rope // 2
    assert rope == 64 and mla_w_ukv.shape[2] == mla_heads * 2 * LANES
    gdn_vh = gdn_a_log.shape[1]
    gdn_dv = gdn_norm.shape[1]
    gdn_dk = gdn_dv
    gdn_vw = gdn_vh * gdn_dv
    gdn_conv_ch = gdn_conv_w.shape[2]
    gdn_qk = (gdn_conv_ch - gdn_vw) // 2
    gdn_kh = gdn_qk // gdn_dk
    xa_dim = LANES
    xa_heads = xa_wq.shape[2] // xa_dim

    inv = 1.0 / (ROPE_THETA ** (jnp.arange(0, rope, 2, dtype=F32) / rope))
    ang = positions.astype(F32).reshape(t, 1) * inv
    cos, sin = jnp.cos(ang), jnp.sin(ang)
    zeros = jnp.zeros((t, 2 * half), F32)
    cos_t = jnp.concatenate([cos, cos, zeros], axis=1)
    sin_t = jnp.concatenate([-sin, sin, zeros], axis=1)

    memn = ln_rows(mem.reshape(batch * mem_len, d), mem_ln_g, mem_ln_b)

    xf = x.reshape(t, d)
    xb = xf.astype(BF16)

    d_ff = ffn_w_gate.shape[-1]
    f_pad = -(-d_ff // FFN_K_TILE) * FFN_K_TILE

    def ffn(xf, xb, l, s):
        h = ffn_gateup(xb, ffn_w_gate, ffn_w_up, l, s, f_pad)
        wd = jnp.pad(ffn_w_down[l, s].astype(BF16), ((0, f_pad - d_ff), (0, 0)))
        return down_ln(h, wd, xf, ln_g[l, 3 * s], ln_b[l, 3 * s], 0.5)

    for l in range(depth):
        i = l // 2
        xf, xb = ffn(xf, xb, l, 0)

        if l % 2 == 0:
            w_in = ab_w_in[i]
            o = 0
            segs = []
            for sz in (q_lora, kv_lora, rope, gla_qk, gla_qk, gla_heads * gla_dv, gate_rank,
                       gla_heads * gla_dv):
                segs.append(w_in[:, o:o + sz])
                o += sz
            w_cq, w_ckv, w_kpe, w_gq, w_gk, w_gv, w_glr, w_r = segs
            w_main = jnp.concatenate([w_cq, w_ckv, w_gq, w_gk, w_gv, w_r], axis=1).astype(BF16)
            w_small = jnp.concatenate(
                [_rope_cols(w_kpe, half), w_glr, jnp.zeros((d, LANES - gate_rank), F32)],
                axis=1).astype(BF16)
            h_main = proj(xb, w_main)
            h_small = proj(xb, w_small)
            q_off = q_lora + kv_lora
            k_off = q_off + gla_qk
            v_off = k_off + gla_qk
            r_off = v_off + gla_heads * gla_dv

            wgu_pad = jnp.pad(gla_w_gate_up[i], ((0, LANES - gate_rank), (0, 0))).astype(BF16)
            kpe, log_a = ab_small(h_small, cos_t, sin_t, wgu_pad, gla_b_gate[i])

            wq = mla_w_uq[i].reshape(q_lora, mla_heads, nope + rope)
            wq = jnp.concatenate(
                [wq[:, :, :nope], wq[:, :, nope:nope + half], wq[:, :, nope + half:],
                 wq[:, :, nope + half:], wq[:, :, nope:nope + half]], axis=2)
            wq = wq.reshape(q_lora, mla_heads * 2 * LANES).astype(BF16)
            q = mla_q_proj(h_main, 0, q_lora, mla_q_norm[i], wq, cos_t, sin_t,
                           float((nope + rope) ** -0.5))
            wkv = mla_w_ukv[i].reshape(kv_lora, mla_heads, 2, LANES)
            wkv = jnp.transpose(wkv, (0, 2, 1, 3)).reshape(kv_lora, 2 * mla_heads * LANES).astype(BF16)
            kv = mla_kv_proj(h_main, q_lora // kv_lora, kv_lora, mla_kv_norm[i], wkv)
            o_mla = mla_attention(q, kv, kpe, batch, mla_heads)

            o_gla = gla_mixer(h_main, log_a, gla_norm[i], batch, gla_heads, gla_dk, gla_dv,
                              q_off, k_off, v_off, r_off)
            y_in = jnp.concatenate([o_mla, o_gla], axis=1)
            w_out = ab_w_out[i].astype(BF16)
        else:
            w_in = c_w_in[i]
            main_w = gdn_conv_ch + gdn_vw
            hc = proj(xb, w_in[:, :main_w].astype(BF16))
            w_small = jnp.pad(w_in[:, main_w:], ((0, 0), (0, LANES - 2 * gdn_vh))).astype(BF16)
            h_small = proj(xb, w_small)
            qkv = gdn_conv(hc, gdn_conv_w[i], batch, gdn_qk, gdn_conv_ch, float(gdn_dk ** -0.5))
            gates = gdn_gates(h_small, gdn_a_log[i], gdn_dt_bias[i])
            ngrp = gdn_vh // GDN_GROUP
            beta_g = gates[:, :gdn_vh].reshape(t, ngrp, GDN_GROUP)
            g_g = gates[:, gdn_vh:2 * gdn_vh].reshape(t, ngrp, GDN_GROUP)
            gg = jnp.concatenate(
                [beta_g, g_g, jnp.zeros((t, ngrp, LANES - 2 * GDN_GROUP), F32)], axis=2)
            gates_c = gg.reshape(t, ngrp * LANES)
            gates_t = jnp.transpose(gg.reshape(t // CHUNK, CHUNK, ngrp * LANES), (0, 2, 1))
            y_in = gdn_mixer(qkv, hc, gates_c, gates_t, gdn_norm[i], batch, gdn_vh, gdn_dk, gdn_dv,
                             gdn_vh // gdn_kh, gdn_conv_ch)
            w_out = c_w_out[i].astype(BF16)

        xf, xb = down_ln(y_in, w_out, xf, ln_g[l, 1], ln_b[l, 1], 1.0)

        qx = proj(xb, xa_wq[l].astype(BF16))
        kvx = proj(memn, xa_wkv[l].astype(BF16), out_dtype=BF16)
        ox = cross_attention(qx, kvx, batch, xa_heads, xa_dim)
        xf, xb = down_ln(ox, xa_wo[l].astype(BF16), xf, ln_g[l, 2], ln_b[l, 2], 1.0)

        xf, xb = ffn(xf, xb, l, 1)

    return xf.reshape(batch, seq, d)
```

```python
import functools

import jax
import jax.numpy as jnp
from jax import lax
from jax.experimental import pallas as pl
from jax.experimental.pallas import tpu as pltpu

F32 = jnp.float32
BF16 = jnp.bfloat16

VMEM_LIMIT_BYTES = 56 * 1024 * 1024
LANES = 128

DEPTH = 2
DN_ALPHA = (2 * DEPTH) ** 0.25
ROPE_THETA = 10000.0
CHUNK = 64
CHUNK_SHIFT = 6
GLA_GATE_NORM = 16.0
LN_EPS = 1e-5
RMS_EPS = 1e-6
L2_EPS = 1e-6
LN_ROWS = 64
FFN_K_TILE = 1024

NT = (((1,), (1,)), ((), ()))
TN = (((0,), (0,)), ((), ()))


def _cparams(*sem):
    return pltpu.CompilerParams(dimension_semantics=sem, vmem_limit_bytes=VMEM_LIMIT_BYTES)


def _dot(a, b, dims=None):
    a = a.astype(BF16)
    b = b.astype(BF16)
    if dims is None:
        return jnp.dot(a, b, preferred_element_type=F32)
    return lax.dot_general(a, b, dims, preferred_element_type=F32)


def _split3(x):
    hi = x.astype(BF16)
    r = x - hi.astype(F32)
    mid = r.astype(BF16)
    lo = (r - mid.astype(F32)).astype(BF16)
    return hi, mid, lo


def _dot_mask(mask_bf16, x, dims=None, mask_first=True):
    out = None
    for piece in _split3(x):
        if mask_first:
            p = (jnp.dot(mask_bf16, piece, preferred_element_type=F32) if dims is None
                 else lax.dot_general(mask_bf16, piece, dims, preferred_element_type=F32))
        else:
            p = (jnp.dot(piece, mask_bf16, preferred_element_type=F32) if dims is None
                 else lax.dot_general(piece, mask_bf16, dims, preferred_element_type=F32))
        out = p if out is None else out + p
    return out


def _dot_hilo(a, b):
    a_hi = a.astype(BF16)
    a_lo = (a - a_hi.astype(F32)).astype(BF16)
    b_hi = b.astype(BF16)
    b_lo = (b - b_hi.astype(F32)).astype(BF16)
    return (jnp.dot(a_hi, b_hi, preferred_element_type=F32)
            + jnp.dot(a_hi, b_lo, preferred_element_type=F32)
            + jnp.dot(a_lo, b_hi, preferred_element_type=F32))


def _silu(x):
    return x * jax.nn.sigmoid(x)


def _layer_norm(y, g, b):
    mu = jnp.mean(y, axis=-1, keepdims=True)
    yc = y - mu
    var = jnp.mean(yc * yc, axis=-1, keepdims=True)
    return yc * lax.rsqrt(var + LN_EPS) * g + b


def _rms_norm(y, g):
    return y * lax.rsqrt(jnp.mean(y * y, axis=-1, keepdims=True) + RMS_EPS) * g


def _pick(n, prefs):
    for p in prefs:
        if n % p == 0:
            return p
    return n


def _ln_rows_kernel(x_ref, g_ref, b_ref, ob_ref):
    ob_ref[...] = _layer_norm(x_ref[...], g_ref[...], b_ref[...]).astype(ob_ref.dtype)


def ln_rows(x, g, b):
    m, d = x.shape
    tm = _pick(m, (256, 128, 64, 32, 16))
    return pl.pallas_call(
        _ln_rows_kernel,
        grid=(m // tm,),
        in_specs=[pl.BlockSpec((tm, d), lambda i: (i, 0)),
                  pl.BlockSpec((1, d), lambda i: (0, 0)),
                  pl.BlockSpec((1, d), lambda i: (0, 0))],
        out_specs=pl.BlockSpec((tm, d), lambda i: (i, 0)),
        out_shape=jax.ShapeDtypeStruct((m, d), BF16),
        compiler_params=_cparams("arbitrary"),
        name="ln_rows",
    )(x, g.reshape(1, d), b.reshape(1, d))


def _gateup_kernel(x_ref, wg_ref, wu_ref, h_ref, wgb_ref, wub_ref, *, n_real):
    j = pl.program_id(0)

    @pl.when(jnp.logical_and(pl.program_id(1) == 0, j < n_real))
    def _():
        wgb_ref[...] = wg_ref[...].astype(BF16)
        wub_ref[...] = wu_ref[...].astype(BF16)

    @pl.when(j < n_real)
    def _():
        x = x_ref[...]
        g = jnp.dot(x, wgb_ref[...], preferred_element_type=F32)
        u = jnp.dot(x, wub_ref[...], preferred_element_type=F32)
        h_ref[...] = (_silu(g) * u).astype(h_ref.dtype)

    @pl.when(j >= n_real)
    def _():
        h_ref[...] = jnp.zeros_like(h_ref)


def ffn_gateup(xb, wg_all, wu_all, l, s, f_pad):
    t, d = xb.shape
    f = wg_all.shape[-1]
    tm = _pick(t, (1024, 512, 256, 128))
    tn = _pick(f, (256, 128))
    n_real = f // tn
    wspec = pl.BlockSpec((None, None, d, tn), lambda j, i: (l, s, 0, jnp.minimum(j, n_real - 1)))
    return pl.pallas_call(
        functools.partial(_gateup_kernel, n_real=n_real),
        grid=(f_pad // tn, t // tm),
        in_specs=[pl.BlockSpec((tm, d), lambda j, i: (i, 0)), wspec, wspec],
        out_specs=pl.BlockSpec((tm, tn), lambda j, i: (i, j)),
        out_shape=jax.ShapeDtypeStruct((t, f_pad), BF16),
        scratch_shapes=[pltpu.VMEM((d, tn), BF16), pltpu.VMEM((d, tn), BF16)],
        compiler_params=_cparams("arbitrary", "arbitrary"),
        name="ffn_gateup",
    )(xb, wg_all, wu_all)


def _down_ln_kernel(a_ref, w_ref, r_ref, g_ref, b_ref, o_ref, ob_ref, *, scale, nk):
    k = pl.program_id(1)

    @pl.when(k == 0)
    def _():
        o_ref[...] = jnp.zeros_like(o_ref)

    o_ref[...] += jnp.dot(a_ref[...], w_ref[...], preferred_element_type=F32)

    @pl.when(k == nk - 1)
    def _():
        g = g_ref[...]
        b = b_ref[...]
        rc = min(LN_ROWS, o_ref.shape[0])

        def body(r, carry):
            rows = pl.ds(pl.multiple_of(r * rc, rc), rc)
            y = DN_ALPHA * r_ref[rows, :] + scale * o_ref[rows, :]
            out = _layer_norm(y, g, b)
            o_ref[rows, :] = out
            ob_ref[rows, :] = out.astype(ob_ref.dtype)
            return carry

        lax.fori_loop(0, o_ref.shape[0] // rc, body, 0)


def down_ln(a, w, resid, g, b, scale, w_idx=()):
    t, kdim = a.shape
    d = w.shape[-1]
    tm = _pick(t, (512, 256, 128))
    tk = _pick(kdim, (FFN_K_TILE, 512, 256, 128))
    nk = kdim // tk
    return pl.pallas_call(
        functools.partial(_down_ln_kernel, scale=scale, nk=nk),
        grid=(t // tm, nk),
        in_specs=[pl.BlockSpec((tm, tk), lambda i, k: (i, k)),
                  pl.BlockSpec((None,) * len(w_idx) + (tk, d), lambda i, k: (*w_idx, k, 0)),
                  pl.BlockSpec((tm, d), lambda i, k: (i, 0), pipeline_mode=pl.Buffered(1)),
                  pl.BlockSpec((1, d), lambda i, k: (0, 0)),
                  pl.BlockSpec((1, d), lambda i, k: (0, 0))],
        out_specs=[pl.BlockSpec((tm, d), lambda i, k: (i, 0)),
                   pl.BlockSpec((tm, d), lambda i, k: (i, 0))],
        out_shape=[jax.ShapeDtypeStruct((t, d), F32), jax.ShapeDtypeStruct((t, d), BF16)],
        compiler_params=_cparams("arbitrary", "arbitrary"),
        name="down_ln",
    )(a, w, resid, g.reshape(1, d), b.reshape(1, d))


def _proj_kernel(a_ref, w_ref, o_ref):
    o_ref[...] = jnp.dot(a_ref[...], w_ref[...], preferred_element_type=F32).astype(o_ref.dtype)


def proj(a, w, out_dtype=F32, n=None):
    t, kdim = a.shape
    n = w.shape[1] if n is None else n
    tm = _pick(t, (1024, 512, 256, 128))
    tn = _pick(n, (1024, 768, 512, 384, 256, 128))
    return pl.pallas_call(
        _proj_kernel,
        grid=(t // tm, n // tn),
        in_specs=[pl.BlockSpec((tm, kdim), lambda i, j: (i, 0)),
                  pl.BlockSpec((kdim, tn), lambda i, j: (0, j))],
        out_specs=pl.BlockSpec((tm, tn), lambda i, j: (i, j)),
        out_shape=jax.ShapeDtypeStruct((t, n), out_dtype),
        compiler_params=_cparams("arbitrary", "arbitrary"),
        name="proj",
    )(a, w)


def _rope128(v, c, s):
    return v * c + pltpu.roll(v, 64, axis=1) * s


def _mla_q_kernel(c_ref, gn_ref, w_ref, cos_ref, sin_ref, q_ref, *, heads_per_tile, scale):
    n = _rms_norm(c_ref[...], gn_ref[...]).astype(BF16)
    q = jnp.dot(n, w_ref[...], preferred_element_type=F32) * scale
    c = cos_ref[...]
    s = sin_ref[...]
    for h in range(heads_per_tile):
        o = h * 2 * LANES
        q_ref[:, o:o + LANES] = q[:, o:o + LANES].astype(q_ref.dtype)
        q_ref[:, o + LANES:o + 2 * LANES] = _rope128(q[:, o + LANES:o + 2 * LANES], c, s).astype(q_ref.dtype)


def mla_q_proj(h_main, col_block, rank, gnorm, w, cos_t, sin_t, scale):
    t = h_main.shape[0]
    n = w.shape[1]
    tm = _pick(t, (512, 256, 128))
    tn = _pick(n, (1024, 512, 256))
    return pl.pallas_call(
        functools.partial(_mla_q_kernel, heads_per_tile=tn // (2 * LANES), scale=scale),
        grid=(t // tm, n // tn),
        in_specs=[pl.BlockSpec((tm, rank), lambda i, j: (i, col_block)),
                  pl.BlockSpec((1, rank), lambda i, j: (0, 0)),
                  pl.BlockSpec((rank, tn), lambda i, j: (0, j)),
                  pl.BlockSpec((tm, LANES), lambda i, j: (i, 0)),
                  pl.BlockSpec((tm, LANES), lambda i, j: (i, 0))],
        out_specs=pl.BlockSpec((tm, tn), lambda i, j: (i, j)),
        out_shape=jax.ShapeDtypeStruct((t, n), BF16),
        compiler_params=_cparams("arbitrary", "arbitrary"),
        name="mla_q_proj",
    )(h_main, gnorm.reshape(1, rank), w, cos_t, sin_t)


def _mla_kv_kernel(c_ref, gn_ref, w_ref, o_ref):
    n = _rms_norm(c_ref[...], gn_ref[...]).astype(BF16)
    o_ref[...] = jnp.dot(n, w_ref[...], preferred_element_type=F32).astype(o_ref.dtype)


def mla_kv_proj(h_main, col_block, rank, gnorm, w):
    t = h_main.shape[0]
    n = w.shape[1]
    tm = _pick(t, (512, 256, 128))
    tn = _pick(n, (1024, 512, 256, 128))
    return pl.pallas_call(
        _mla_kv_kernel,
        grid=(t // tm, n // tn),
        in_specs=[pl.BlockSpec((tm, rank), lambda i, j: (i, col_block)),
                  pl.BlockSpec((1, rank), lambda i, j: (0, 0)),
                  pl.BlockSpec((rank, tn), lambda i, j: (0, j))],
        out_specs=pl.BlockSpec((tm, tn), lambda i, j: (i, j)),
        out_shape=jax.ShapeDtypeStruct((t, n), BF16),
        compiler_params=_cparams("arbitrary", "arbitrary"),
        name="mla_kv_proj",
    )(h_main, gnorm.reshape(1, rank), w)


def _ab_small_kernel(hs_ref, cos_ref, sin_ref, wgu_ref, bg_ref, kpe_ref, la_ref):
    hs = hs_ref[...]
    kpe_ref[...] = _rope128(hs[:, :LANES], cos_ref[...], sin_ref[...]).astype(kpe_ref.dtype)
    z = jnp.dot(hs[:, LANES:].astype(BF16), wgu_ref[...], preferred_element_type=F32) + bg_ref[...]
    log_sig = jnp.minimum(z, 0.0) - jnp.log1p(jnp.exp(-jnp.abs(z)))
    la_ref[...] = log_sig / GLA_GATE_NORM


def ab_small(hs, cos_t, sin_t, wgu_pad, b_gate):
    t = hs.shape[0]
    n = wgu_pad.shape[1]
    tm = _pick(t, (512, 256, 128))
    return pl.pallas_call(
        _ab_small_kernel,
        grid=(t // tm,),
        in_specs=[pl.BlockSpec((tm, 2 * LANES), lambda i: (i, 0)),
                  pl.BlockSpec((tm, LANES), lambda i: (i, 0)),
                  pl.BlockSpec((tm, LANES), lambda i: (i, 0)),
                  pl.BlockSpec((LANES, n), lambda i: (0, 0)),
                  pl.BlockSpec((1, n), lambda i: (0, 0))],
        out_specs=[pl.BlockSpec((tm, LANES), lambda i: (i, 0)),
                   pl.BlockSpec((tm, n), lambda i: (i, 0))],
        out_shape=[jax.ShapeDtypeStruct((t, LANES), BF16), jax.ShapeDtypeStruct((t, n), F32)],
        compiler_params=_cparams("arbitrary"),
        name="ab_small",
    )(hs, cos_t, sin_t, wgu_pad, b_gate.reshape(1, n))


ATTN_HEADS = 4


def _mla_attn_kernel(q_ref, kn_ref, kpe_ref, v_ref, o_ref, m_ref, l_ref, acc_ref, *, tq, nh):
    qi = pl.program_id(2)
    heads = range(nh)
    qs = [q_ref[:, h * 2 * LANES:(h + 1) * 2 * LANES] for h in heads]
    m_ref[...] = jnp.full_like(m_ref, -jnp.inf)
    l_ref[...] = jnp.zeros_like(l_ref)
    acc_ref[...] = jnp.zeros_like(acc_ref)

    def step(ki, masked):
        rows = pl.ds(pl.multiple_of(ki * tq, tq), tq)
        kpe = kpe_ref[rows, :]
        ss = [lax.dot_general(qs[h], jnp.concatenate([kn_ref[rows, h * LANES:(h + 1) * LANES], kpe], axis=1),
                              NT, preferred_element_type=F32) for h in heads]
        if masked:
            row = lax.broadcasted_iota(jnp.int32, (tq, tq), 0)
            col = lax.broadcasted_iota(jnp.int32, (tq, tq), 1)
            ss = [jnp.where(col <= row, s, -jnp.inf) for s in ss]
        m_prev = [m_ref[h] for h in heads]
        m_new = [jnp.maximum(m_prev[h], jnp.max(ss[h], axis=-1, keepdims=True)) for h in heads]
        ps = [jnp.exp(ss[h] - m_new[h]) for h in heads]
        alpha = [jnp.exp(m_prev[h] - m_new[h]) for h in heads]
        pv = [jnp.dot(ps[h].astype(BF16), v_ref[rows, h * LANES:(h + 1) * LANES],
                      preferred_element_type=F32) for h in heads]
        for h in heads:
            l_ref[h] = alpha[h] * l_ref[h] + jnp.sum(ps[h], axis=-1, keepdims=True)
            acc_ref[h] = alpha[h] * acc_ref[h] + pv[h]
            m_ref[h] = m_new[h]

    def body(ki, carry):
        step(ki, False)
        return carry

    lax.fori_loop(0, qi, body, 0)
    step(qi, True)
    for h in heads:
        o_ref[:, h * LANES:(h + 1) * LANES] = (acc_ref[h] / l_ref[h]).astype(o_ref.dtype)


def mla_attention(q, kv, kpe, batch, heads):
    t = q.shape[0]
    s = t // batch
    tq = _pick(s, (512, 256, 128))
    nq = s // tq
    nh = _pick(heads, (ATTN_HEADS, 2, 1))
    ng = heads // nh
    return pl.pallas_call(
        functools.partial(_mla_attn_kernel, tq=tq, nh=nh),
        grid=(batch, ng, nq),
        in_specs=[
            pl.BlockSpec((tq, nh * 2 * LANES), lambda b, g, qi: (b * nq + qi, g)),
            pl.BlockSpec((s, nh * LANES), lambda b, g, qi: (b, g)),
            pl.BlockSpec((s, LANES), lambda b, g, qi: (b, 0)),
            pl.BlockSpec((s, nh * LANES), lambda b, g, qi: (b, ng + g)),
        ],
        out_specs=pl.BlockSpec((tq, nh * LANES), lambda b, g, qi: (b * nq + qi, g)),
        out_shape=jax.ShapeDtypeStruct((t, heads * LANES), BF16),
        scratch_shapes=[pltpu.VMEM((nh, tq, 1), F32), pltpu.VMEM((nh, tq, 1), F32),
                        pltpu.VMEM((nh, tq, LANES), F32)],
        compiler_params=_cparams("arbitrary", "arbitrary", "arbitrary"),
        name="mla_attention",
    )(q, kv, kpe, kv)


GLA_CHUNKS = 4


def _gla_kernel(q_ref, k_ref, v_ref, la_ref, r_ref, gn_ref, o_ref, st_ref, *, dk, nc):
    c = pl.program_id(2)

    @pl.when(c == 0)
    def _():
        st_ref[...] = jnp.zeros_like(st_ref)

    rows = nc * CHUNK
    ri = lax.broadcasted_iota(jnp.int32, (rows, rows), 0)
    ci = lax.broadcasted_iota(jnp.int32, (rows, rows), 1)
    same_chunk = (ri >> CHUNK_SHIFT) == (ci >> CHUNK_SHIFT)
    tril = jnp.logical_and(same_chunk, ci <= ri)
    la = la_ref[...]
    bcum = _dot_mask(tril.astype(BF16), la)
    b_end = _dot_mask(same_chunk.astype(BF16), la)
    q_in = (q_ref[...] * (dk ** -0.5) * jnp.exp(bcum)).astype(BF16)
    k = k_ref[...]
    v = v_ref[...].astype(BF16)
    k_in = k * jnp.exp(-bcum)
    k_tail = (k * jnp.exp(b_end - bcum)).astype(BF16)
    attn = jnp.where(tril, _dot(q_in, k_in, NT), 0.0)
    o_intra = _dot(attn, v)
    decay = jnp.exp(b_end)
    gn = gn_ref[...]
    st = st_ref[...]
    for j in range(nc):
        sl = slice(j * CHUNK, (j + 1) * CHUNK)
        o = o_intra[sl] + _dot(q_in[sl], st, NT)
        st = st * decay[j * CHUNK:j * CHUNK + 1, :] + _dot(v[sl], k_tail[sl], TN)
        o_ref[sl, :] = (_rms_norm(o, gn) * _silu(r_ref[sl, :])).astype(o_ref.dtype)
    st_ref[...] = st


def gla_mixer(h_main, log_a, gnorm, batch, heads, dk, dv, q_off, k_off, v_off, r_off):
    t = h_main.shape[0]
    nc = _pick(t // batch // CHUNK, (GLA_CHUNKS, 2, 1))
    rows = nc * CHUNK
    n = t // batch // rows
    return pl.pallas_call(
        functools.partial(_gla_kernel, dk=dk, nc=nc),
        grid=(batch, heads, n),
        in_specs=[
            pl.BlockSpec((rows, dk), lambda b, h, c: (b * n + c, q_off // dk + h)),
            pl.BlockSpec((rows, dk), lambda b, h, c: (b * n + c, k_off // dk + h)),
            pl.BlockSpec((rows, dv), lambda b, h, c: (b * n + c, v_off // dv + h)),
            pl.BlockSpec((rows, dk), lambda b, h, c: (b * n + c, h)),
            pl.BlockSpec((rows, dv), lambda b, h, c: (b * n + c, r_off // dv + h)),
            pl.BlockSpec((1, dv), lambda b, h, c: (0, 0)),
        ],
        out_specs=pl.BlockSpec((rows, dv), lambda b, h, c: (b * n + c, h)),
        out_shape=jax.ShapeDtypeStruct((t, heads * dv), BF16),
        scratch_shapes=[pltpu.VMEM((dv, dk), F32)],
        compiler_params=_cparams("arbitrary", "arbitrary", "arbitrary"),
        name="gla_mixer",
    )(h_main, h_main, h_main, log_a, h_main, gnorm.reshape(1, dv))


CONV_ROWS = 512


def _conv_kernel(x_ref, w_ref, o_ref, *, seq, conv_k, n_qk_blocks, q_blocks, q_scale, tc):
    j = pl.program_id(1)
    w = w_ref[...]
    rows = min(CONV_ROWS, seq)

    def conv_chunk(r0, first):
        cur = x_ref[pl.ds(r0, rows), :]
        if first:
            prev = jnp.zeros((8, tc), F32)
        else:
            prev = x_ref[pl.ds(r0 - 8, 8), :]
        ext = jnp.concatenate([prev, cur], axis=0)
        acc = cur * w[conv_k - 1:conv_k, :]
        for d in range(1, conv_k):
            acc = acc + pltpu.roll(ext, d, axis=0)[8:, :] * w[conv_k - 1 - d:conv_k - d, :]
        return _silu(acc)

    def l2n(y):
        parts = []
        for g in range(tc // LANES):
            yg = y[:, g * LANES:(g + 1) * LANES]
            parts.append(yg * lax.rsqrt(jnp.sum(yg * yg, axis=-1, keepdims=True) + L2_EPS))
        return jnp.concatenate(parts, axis=1) if len(parts) > 1 else parts[0]

    for ci in range(seq // rows):
        r0 = ci * rows
        y = conv_chunk(r0, ci == 0)

        @pl.when(j < q_blocks)
        def _():
            o_ref[pl.ds(r0, rows), :] = l2n(y) * q_scale

        @pl.when(jnp.logical_and(j >= q_blocks, j < n_qk_blocks))
        def _():
            o_ref[pl.ds(r0, rows), :] = l2n(y)

        @pl.when(j >= n_qk_blocks)
        def _():
            o_ref[pl.ds(r0, rows), :] = y


def gdn_conv(hc, conv_w, batch, qk_width, conv_ch, q_scale):
    t = hc.shape[0]
    seq = t // batch
    conv_k = conv_w.shape[0]
    tc = 256
    return pl.pallas_call(
        functools.partial(_conv_kernel, seq=seq, conv_k=conv_k, n_qk_blocks=2 * qk_width // tc,
                          q_blocks=qk_width // tc, q_scale=q_scale, tc=tc),
        grid=(batch, conv_ch // tc),
        in_specs=[pl.BlockSpec((seq, tc), lambda b, j: (b, j)),
                  pl.BlockSpec((conv_k, tc), lambda b, j: (0, j))],
        out_specs=pl.BlockSpec((seq, tc), lambda b, j: (b, j)),
        out_shape=jax.ShapeDtypeStruct((t, conv_ch), F32),
        compiler_params=_cparams("arbitrary", "arbitrary"),
        name="gdn_conv",
    )(hc, conv_w)


def _gdn_gates_kernel(hs_ref, negA_ref, dt_ref, o_ref, *, nh):
    hs = hs_ref[...]
    beta = jax.nn.sigmoid(hs)
    z = pltpu.roll(hs, LANES - nh, axis=1) + dt_ref[...]
    sp = jnp.maximum(z, 0.0) + jnp.log1p(jnp.exp(-jnp.abs(z)))
    g = negA_ref[...] * sp
    lane = lax.broadcasted_iota(jnp.int32, hs.shape, 1)
    o_ref[...] = jnp.where(lane < nh, beta, pltpu.roll(g, nh, axis=1))


def gdn_gates(hs, a_log, dt_bias):
    t = hs.shape[0]
    nh = a_log.shape[0]
    tm = _pick(t, (1024, 512, 256, 128))
    pad = LANES - nh
    neg_a = jnp.pad(-jnp.exp(a_log.astype(F32)), (0, pad)).reshape(1, LANES)
    dt = jnp.pad(dt_bias.astype(F32), (0, pad)).reshape(1, LANES)
    return pl.pallas_call(
        functools.partial(_gdn_gates_kernel, nh=nh),
        grid=(t // tm,),
        in_specs=[pl.BlockSpec((tm, LANES), lambda i: (i, 0)),
                  pl.BlockSpec((1, LANES), lambda i: (0, 0)),
                  pl.BlockSpec((1, LANES), lambda i: (0, 0))],
        out_specs=pl.BlockSpec((tm, LANES), lambda i: (i, 0)),
        out_shape=jax.ShapeDtypeStruct((t, LANES), F32),
        compiler_params=_cparams("arbitrary"),
        name="gdn_gates",
    )(hs, neg_a, dt)


GDN_GROUP = 8
INV_BLOCK_SHIFTS = (3, 4, 5)


def _unit_lower_inverses(ms, same_blk, eye):
    m0 = [jnp.where(same_blk[0], m, 0.0) for m in ms]
    m2 = [_dot(x, x) for x in m0]
    a = [eye - x for x in m0]
    t = [_dot(x, y) for x, y in zip(a, m2)]
    m4 = [_dot(x, x) for x in m2]
    a = [x + y for x, y in zip(a, t)]
    t = [_dot(x, y) for x, y in zip(a, m4)]
    a = [x + y for x, y in zip(a, t)]
    inner = same_blk[0]
    for outer in same_blk[1:] + [None]:
        if outer is None:
            sel = jnp.logical_not(inner)
        else:
            sel = jnp.logical_and(outer, jnp.logical_not(inner))
            inner = outer
        ea = [_dot(jnp.where(sel, m, 0.0), x) for m, x in zip(ms, a)]
        t = [_dot(x, y) for x, y in zip(a, ea)]
        a = [x - y for x, y in zip(a, t)]
    return a


def _gdn_kernel(q_ref, k_ref, v_ref, z_ref, gcol_ref, grow_ref, gn_ref, o_ref, st_ref,
                *, nh, dk, dv, rep):
    c = pl.program_id(2)

    @pl.when(c == 0)
    def _():
        st_ref[...] = jnp.zeros_like(st_ref)

    ri = lax.broadcasted_iota(jnp.int32, (CHUNK, CHUNK), 0)
    ci = lax.broadcasted_iota(jnp.int32, (CHUNK, CHUNK), 1)
    incl = ci <= ri
    strict = ci < ri
    gcol = gcol_ref[...]
    grow = grow_ref[0]
    gc_cols = _dot_mask(incl.astype(BF16), gcol)
    gc_rows = _dot_mask((ri <= ci).astype(BF16), grow, mask_first=False)
    gnorm = gn_ref[...]
    same_blk = [(ri >> sh) == (ci >> sh) for sh in INV_BLOCK_SHIFTS]
    eye = (ri == ci).astype(F32)

    heads = range(nh)

    ks = [k_ref[:, kh * dk:(kh + 1) * dk] for kh in range(nh // rep)]
    qs = [q_ref[:, kh * dk:(kh + 1) * dk] for kh in range(nh // rep)]
    kk = [_dot(k, k, NT) for k in ks]
    qk = [_dot(q, k, NT) for q, k in zip(qs, ks)]

    beta = [gcol[:, hh:hh + 1] for hh in heads]
    gc_c = [gc_cols[:, nh + hh:nh + hh + 1] for hh in heads]
    gc_r = [gc_rows[nh + hh:nh + hh + 1, :] for hh in heads]
    gc_last = [g[CHUNK - 1:CHUNK, :] for g in gc_c]
    egc = [jnp.exp(g) for g in gc_c]
    decay = [jnp.exp(jnp.where(incl, gc_c[hh] - gc_r[hh], -jnp.inf)) for hh in heads]
    ms = [jnp.where(strict, beta[hh] * kk[hh // rep] * decay[hh], 0.0) for hh in heads]
    rhs = [jnp.concatenate([v_ref[:, hh * dv:(hh + 1) * dv] * beta[hh],
                            ks[hh // rep] * (beta[hh] * egc[hh])], axis=1) for hh in heads]
    inv = _unit_lower_inverses(ms, same_blk, eye)
    x = [_dot(a, r) for a, r in zip(inv, rhs)]
    st = [st_ref[hh] for hh in heads]
    ws = [_dot(x[hh][:, dv:], st[hh]) for hh in heads]
    o_st = [_dot(qs[hh // rep] * egc[hh], st[hh]) for hh in heads]
    v_new = [x[hh][:, :dv] - ws[hh] for hh in heads]
    o_in = [_dot(qk[hh // rep] * decay[hh], v_new[hh]) for hh in heads]
    kv = [_dot(ks[hh // rep] * jnp.exp(gc_last[hh] - gc_c[hh]), v_new[hh], TN) for hh in heads]
    for hh in heads:
        st_ref[hh] = jnp.exp(gc_last[hh]) * st[hh] + kv[hh]
        zz = z_ref[:, hh * dv:(hh + 1) * dv]
        o = o_st[hh] + o_in[hh]
        o_ref[:, hh * dv:(hh + 1) * dv] = (_rms_norm(o, gnorm) * _silu(zz)).astype(o_ref.dtype)


def gdn_mixer(qkv, hc, gates, gates_t, gnorm, batch, nh, dk, dv, rep, z_off):
    t = qkv.shape[0]
    n = t // batch // CHUNK
    ngrp = nh // GDN_GROUP
    kw = (GDN_GROUP // rep) * dk
    vw = GDN_GROUP * dv
    nkh = nh // rep
    k_blk0 = nkh * dk // kw
    v_blk0 = 2 * nkh * dk // vw
    z_blk0 = z_off // vw
    return pl.pallas_call(
        functools.partial(_gdn_kernel, nh=GDN_GROUP, dk=dk, dv=dv, rep=rep),
        grid=(batch, ngrp, n),
        in_specs=[
            pl.BlockSpec((CHUNK, kw), lambda b, g, c: (b * n + c, g)),
            pl.BlockSpec((CHUNK, kw), lambda b, g, c: (b * n + c, k_blk0 + g)),
            pl.BlockSpec((CHUNK, vw), lambda b, g, c: (b * n + c, v_blk0 + g)),
            pl.BlockSpec((CHUNK, vw), lambda b, g, c: (b * n + c, z_blk0 + g)),
            pl.BlockSpec((CHUNK, LANES), lambda b, g, c: (b * n + c, g)),
            pl.BlockSpec((1, LANES, CHUNK), lambda b, g, c: (b * n + c, g, 0)),
            pl.BlockSpec((1, dv), lambda b, g, c: (0, 0)),
        ],
        out_specs=pl.BlockSpec((CHUNK, vw), lambda b, g, c: (b * n + c, g)),
        out_shape=jax.ShapeDtypeStruct((t, nh * dv), BF16),
        scratch_shapes=[pltpu.VMEM((GDN_GROUP, dk, dv), F32)],
        compiler_params=_cparams("arbitrary", "arbitrary", "arbitrary"),
        name="gdn_mixer",
    )(qkv, qkv, qkv, hc, gates, gates_t, gnorm.reshape(1, dv))


def _xattn_kernel(q_ref, kv_ref, o_ref, *, heads, dim):
    scale = dim ** -0.5
    for h in range(heads):
        q = (q_ref[:, h * dim:(h + 1) * dim] * scale).astype(BF16)
        k = kv_ref[:, h * dim:(h + 1) * dim]
        v = kv_ref[:, (heads + h) * dim:(heads + h + 1) * dim]
        s = lax.dot_general(q, k, NT, preferred_element_type=F32)
        s = s - jnp.max(s, axis=-1, keepdims=True)
        p = jnp.exp(s)
        p = p / jnp.sum(p, axis=-1, keepdims=True)
        o_ref[:, h * dim:(h + 1) * dim] = jnp.dot(p.astype(BF16), v,
                                                  preferred_element_type=F32).astype(o_ref.dtype)


def cross_attention(q, kv, batch, heads, dim):
    t = q.shape[0]
    s = t // batch
    m = kv.shape[0] // batch
    tq = _pick(s, (512, 256, 128))
    nq = s // tq
    return pl.pallas_call(
        functools.partial(_xattn_kernel, heads=heads, dim=dim),
        grid=(batch, nq),
        in_specs=[pl.BlockSpec((tq, heads * dim), lambda b, i: (b * nq + i, 0)),
                  pl.BlockSpec((m, 2 * heads * dim), lambda b, i: (b, 0))],
        out_specs=pl.BlockSpec((tq, heads * dim), lambda b, i: (b * nq + i, 0)),
        out_shape=jax.ShapeDtypeStruct((t, heads * dim), BF16),
        compiler_params=_cparams("arbitrary", "arbitrary"),
        name="cross_attention",
    )(q, kv)


def _rope_cols(w, half):
    x1, x2 = w[:, :half], w[:, half:2 * half]
    return jnp.concatenate([x1, x2, x2, x1], axis=1)


def kernel(x, mem, positions, mem_ln_g, mem_ln_b, ln_g, ln_b, ffn_w_gate, ffn_w_up, ffn_w_down,
           ab_w_in, mla_q_norm, mla_w_uq, mla_kv_norm, mla_w_ukv, gla_w_gate_up, gla_b_gate, gla_norm,
           ab_w_out, c_w_in, gdn_conv_w, gdn_a_log, gdn_dt_bias, gdn_norm, c_w_out,
           xa_wq, xa_wkv, xa_wo):
    batch, seq, d = x.shape
    t = batch * seq
    mem_len = mem.shape[1]
    depth = ln_g.shape[0]

    q_lora = mla_q_norm.shape[1]
    kv_lora = mla_kv_norm.shape[1]
    gla_dv = gla_norm.shape[1]
    gla_dk = gla_dv // 2
    gla_qk = gla_w_gate_up.shape[2]
    gla_heads = gla_qk // gla_dk
    gate_rank = gla_w_gate_up.shape[1]
    mla_heads = (ab_w_out.shape[1] - gla_heads * gla_dv) // LANES
    nope = LANES
    rope = mla_w_uq.shape[2] // mla_heads - nope
    half = rope // 2
    assert rope == 64 and mla_w_ukv.shape[2] == mla_heads * 2 * LANES
    gdn_vh = gdn_a_log.shape[1]
    gdn_dv = gdn_norm.shape[1]
    gdn_dk = gdn_dv
    gdn_vw = gdn_vh * gdn_dv
    gdn_conv_ch = gdn_conv_w.shape[2]
    gdn_qk = (gdn_conv_ch - gdn_vw) // 2
    gdn_kh = gdn_qk // gdn_dk
    xa_dim = LANES
    xa_heads = xa_wq.shape[2] // xa_dim

    inv = 1.0 / (ROPE_THETA ** (jnp.arange(0, rope, 2, dtype=F32) / rope))
    ang = positions.astype(F32).reshape(t, 1) * inv
    cos, sin = jnp.cos(ang), jnp.sin(ang)
    zeros = jnp.zeros((t, 2 * half), F32)
    cos_t = jnp.concatenate([cos, cos, zeros], axis=1)
    sin_t = jnp.concatenate([-sin, sin, zeros], axis=1)

    memn = ln_rows(mem.reshape(batch * mem_len, d), mem_ln_g, mem_ln_b)

    xf = x.reshape(t, d)
    xb = xf.astype(BF16)

    d_ff = ffn_w_gate.shape[-1]
    f_pad = -(-d_ff // FFN_K_TILE) * FFN_K_TILE

    wd_all = jnp.pad(ffn_w_down.astype(BF16), ((0, 0), (0, 0), (0, f_pad - d_ff), (0, 0)))

    def ffn(xf, xb, l, s):
        h = ffn_gateup(xb, ffn_w_gate, ffn_w_up, l, s, f_pad)
        return down_ln(h, wd_all, xf, ln_g[l, 3 * s], ln_b[l, 3 * s], 0.5, w_idx=(l, s))

    for l in range(depth):
        i = l // 2
        xf, xb = ffn(xf, xb, l, 0)

        if l % 2 == 0:
            w_in = ab_w_in[i].astype(BF16)
            o = 0
            segs = []
            for sz in (q_lora, kv_lora, rope, gla_qk, gla_qk, gla_heads * gla_dv, gate_rank,
                       gla_heads * gla_dv):
                segs.append(w_in[:, o:o + sz])
                o += sz
            w_cq, w_ckv, w_kpe, w_gq, w_gk, w_gv, w_glr, w_r = segs
            w_main = jnp.concatenate([w_cq, w_ckv, w_gq, w_gk, w_gv, w_r], axis=1)
            w_small = jnp.concatenate(
                [_rope_cols(w_kpe, half), w_glr, jnp.zeros((d, LANES - gate_rank), BF16)], axis=1)
            h_main = proj(xb, w_main)
            h_small = proj(xb, w_small)
            q_off = q_lora + kv_lora
            k_off = q_off + gla_qk
            v_off = k_off + gla_qk
            r_off = v_off + gla_heads * gla_dv

            wgu_pad = jnp.pad(gla_w_gate_up[i], ((0, LANES - gate_rank), (0, 0))).astype(BF16)
            kpe, log_a = ab_small(h_small, cos_t, sin_t, wgu_pad, gla_b_gate[i])

            wq = mla_w_uq[i].reshape(q_lora, mla_heads, nope + rope)
            wq = jnp.concatenate(
                [wq[:, :, :nope], wq[:, :, nope:nope + half], wq[:, :, nope + half:],
                 wq[:, :, nope + half:], wq[:, :, nope:nope + half]], axis=2)
            wq = wq.reshape(q_lora, mla_heads * 2 * LANES).astype(BF16)
            q = mla_q_proj(h_main, 0, q_lora, mla_q_norm[i], wq, cos_t, sin_t,
                           float((nope + rope) ** -0.5))
            wkv = mla_w_ukv[i].reshape(kv_lora, mla_heads, 2, LANES)
            wkv = jnp.transpose(wkv, (0, 2, 1, 3)).reshape(kv_lora, 2 * mla_heads * LANES).astype(BF16)
            kv = mla_kv_proj(h_main, q_lora // kv_lora, kv_lora, mla_kv_norm[i], wkv)
            o_mla = mla_attention(q, kv, kpe, batch, mla_heads)

            o_gla = gla_mixer(h_main, log_a, gla_norm[i], batch, gla_heads, gla_dk, gla_dv,
                              q_off, k_off, v_off, r_off)
            y_in = jnp.concatenate([o_mla, o_gla], axis=1)
            w_out = ab_w_out[i].astype(BF16)
        else:
            w_in = c_w_in[i].astype(BF16)
            main_w = gdn_conv_ch + gdn_vw
            hc = proj(xb, w_in, n=main_w)
            w_small = jnp.pad(w_in[:, main_w:], ((0, 0), (0, LANES - 2 * gdn_vh)))
            h_small = proj(xb, w_small)
            qkv = gdn_conv(hc, gdn_conv_w[i], batch, gdn_qk, gdn_conv_ch, float(gdn_dk ** -0.5))
            gates = gdn_gates(h_small, gdn_a_log[i], gdn_dt_bias[i])
            ngrp = gdn_vh // GDN_GROUP
            beta_g = gates[:, :gdn_vh].reshape(t, ngrp, GDN_GROUP)
            g_g = gates[:, gdn_vh:2 * gdn_vh].reshape(t, ngrp, GDN_GROUP)
            gg = jnp.concatenate(
                [beta_g, g_g, jnp.zeros((t, ngrp, LANES - 2 * GDN_GROUP), F32)], axis=2)
            gates_c = gg.reshape(t, ngrp * LANES)
            gates_t = jnp.transpose(gg.reshape(t // CHUNK, CHUNK, ngrp * LANES), (0, 2, 1))
            y_in = gdn_mixer(qkv, hc, gates_c, gates_t, gdn_norm[i], batch, gdn_vh, gdn_dk, gdn_dv,
                             gdn_vh // gdn_kh, gdn_conv_ch)
            w_out = c_w_out[i].astype(BF16)

        xf, xb = down_ln(y_in, w_out, xf, ln_g[l, 1], ln_b[l, 1], 1.0)

        qx = proj(xb, xa_wq[l].astype(BF16))
        kvx = proj(memn, xa_wkv[l].astype(BF16), out_dtype=BF16)
        ox = cross_attention(qx, kvx, batch, xa_heads, xa_dim)
        xf, xb = down_ln(ox, xa_wo[l].astype(BF16), xf, ln_g[l, 2], ln_b[l, 2], 1.0)

        xf, xb = ffn(xf, xb, l, 1)

    return xf.reshape(batch, seq, d)
```

```python
import functools

import jax
import jax.numpy as jnp
from jax import lax
from jax.experimental import pallas as pl
from jax.experimental.pallas import tpu as pltpu

F32 = jnp.float32
BF16 = jnp.bfloat16

VMEM_LIMIT_BYTES = 56 * 1024 * 1024
LANES = 128

DEPTH = 2
DN_ALPHA = (2 * DEPTH) ** 0.25
ROPE_THETA = 10000.0
CHUNK = 64
CHUNK_SHIFT = 6
GLA_GATE_NORM = 16.0
LN_EPS = 1e-5
RMS_EPS = 1e-6
L2_EPS = 1e-6
LN_ROWS = 64
LN_COLS = 512
FFN_K_TILE = 1024

NT = (((1,), (1,)), ((), ()))
TN = (((0,), (0,)), ((), ()))


def _cparams(*sem):
    return pltpu.CompilerParams(dimension_semantics=sem, vmem_limit_bytes=VMEM_LIMIT_BYTES)


def _dot(a, b, dims=None):
    a = a.astype(BF16)
    b = b.astype(BF16)
    if dims is None:
        return jnp.dot(a, b, preferred_element_type=F32)
    return lax.dot_general(a, b, dims, preferred_element_type=F32)


def _split3(x):
    hi = x.astype(BF16)
    r = x - hi.astype(F32)
    mid = r.astype(BF16)
    lo = (r - mid.astype(F32)).astype(BF16)
    return hi, mid, lo


def _dot_mask(mask_bf16, x, dims=None, mask_first=True):
    out = None
    for piece in _split3(x):
        if mask_first:
            p = (jnp.dot(mask_bf16, piece, preferred_element_type=F32) if dims is None
                 else lax.dot_general(mask_bf16, piece, dims, preferred_element_type=F32))
        else:
            p = (jnp.dot(piece, mask_bf16, preferred_element_type=F32) if dims is None
                 else lax.dot_general(piece, mask_bf16, dims, preferred_element_type=F32))
        out = p if out is None else out + p
    return out


def _dot_hilo(a, b):
    a_hi = a.astype(BF16)
    a_lo = (a - a_hi.astype(F32)).astype(BF16)
    b_hi = b.astype(BF16)
    b_lo = (b - b_hi.astype(F32)).astype(BF16)
    return (jnp.dot(a_hi, b_hi, preferred_element_type=F32)
            + jnp.dot(a_hi, b_lo, preferred_element_type=F32)
            + jnp.dot(a_lo, b_hi, preferred_element_type=F32))


def _silu(x):
    return x * jax.nn.sigmoid(x)


def _layer_norm(y, g, b):
    mu = jnp.mean(y, axis=-1, keepdims=True)
    yc = y - mu
    var = jnp.mean(yc * yc, axis=-1, keepdims=True)
    return yc * lax.rsqrt(var + LN_EPS) * g + b


def _rms_norm(y, g):
    return y * lax.rsqrt(jnp.mean(y * y, axis=-1, keepdims=True) + RMS_EPS) * g


def _pick(n, prefs):
    for p in prefs:
        if n % p == 0:
            return p
    return n


def _ln_rows_kernel(x_ref, g_ref, b_ref, ob_ref):
    ob_ref[...] = _layer_norm(x_ref[...], g_ref[...], b_ref[...]).astype(ob_ref.dtype)


def ln_rows(x, g, b):
    m, d = x.shape
    tm = _pick(m, (256, 128, 64, 32, 16))
    return pl.pallas_call(
        _ln_rows_kernel,
        grid=(m // tm,),
        in_specs=[pl.BlockSpec((tm, d), lambda i: (i, 0)),
                  pl.BlockSpec((1, d), lambda i: (0, 0)),
                  pl.BlockSpec((1, d), lambda i: (0, 0))],
        out_specs=pl.BlockSpec((tm, d), lambda i: (i, 0)),
        out_shape=jax.ShapeDtypeStruct((m, d), BF16),
        compiler_params=_cparams("arbitrary"),
        name="ln_rows",
    )(x, g.reshape(1, d), b.reshape(1, d))


def _gateup_kernel(x_ref, wg_ref, wu_ref, h_ref, wgb_ref, wub_ref, *, n_real):
    j = pl.program_id(0)

    @pl.when(jnp.logical_and(pl.program_id(1) == 0, j < n_real))
    def _():
        wgb_ref[...] = wg_ref[...].astype(BF16)
        wub_ref[...] = wu_ref[...].astype(BF16)

    @pl.when(j < n_real)
    def _():
        x = x_ref[...]
        g = jnp.dot(x, wgb_ref[...], preferred_element_type=F32)
        u = jnp.dot(x, wub_ref[...], preferred_element_type=F32)
        h_ref[...] = (_silu(g) * u).astype(h_ref.dtype)

    @pl.when(j >= n_real)
    def _():
        h_ref[...] = jnp.zeros_like(h_ref)


def ffn_gateup(xb, wg_all, wu_all, l, s, f_pad):
    t, d = xb.shape
    f = wg_all.shape[-1]
    tm = _pick(t, (1024, 512, 256, 128))
    tn = _pick(f, (256, 128))
    n_real = f // tn
    wspec = pl.BlockSpec((None, None, d, tn), lambda j, i: (l, s, 0, jnp.minimum(j, n_real - 1)))
    return pl.pallas_call(
        functools.partial(_gateup_kernel, n_real=n_real),
        grid=(f_pad // tn, t // tm),
        in_specs=[pl.BlockSpec((tm, d), lambda j, i: (i, 0)), wspec, wspec],
        out_specs=pl.BlockSpec((tm, tn), lambda j, i: (i, j)),
        out_shape=jax.ShapeDtypeStruct((t, f_pad), BF16),
        scratch_shapes=[pltpu.VMEM((d, tn), BF16), pltpu.VMEM((d, tn), BF16)],
        compiler_params=_cparams("arbitrary", "arbitrary"),
        name="ffn_gateup",
    )(xb, wg_all, wu_all)


CAST_ROWS = 256


def _cast_pad_kernel(w_ref, o_ref, *, n_real):
    j = pl.program_id(2)

    @pl.when(j < n_real)
    def _():
        o_ref[...] = w_ref[...].astype(o_ref.dtype)

    @pl.when(j >= n_real)
    def _():
        o_ref[...] = jnp.zeros_like(o_ref)


def cast_pad_rows(w, rows_pad):
    nl, ns, kdim, d = w.shape
    tr = _pick(kdim, (CAST_ROWS, 128))
    n_real = kdim // tr
    return pl.pallas_call(
        functools.partial(_cast_pad_kernel, n_real=n_real),
        grid=(nl, ns, rows_pad // tr),
        in_specs=[pl.BlockSpec((None, None, tr, d), lambda l, s, j: (l, s, jnp.minimum(j, n_real - 1), 0))],
        out_specs=pl.BlockSpec((None, None, tr, d), lambda l, s, j: (l, s, j, 0)),
        out_shape=jax.ShapeDtypeStruct((nl, ns, rows_pad, d), BF16),
        compiler_params=_cparams("arbitrary", "arbitrary", "arbitrary"),
        name="cast_pad_rows",
    )(w)


def _down_ln_kernel(a_ref, w_ref, r_ref, g_ref, b_ref, o_ref, ob_ref, *, scale, nk):
    k = pl.program_id(1)

    @pl.when(k == 0)
    def _():
        o_ref[...] = jnp.zeros_like(o_ref)

    o_ref[...] += jnp.dot(a_ref[...], w_ref[...], preferred_element_type=F32)

    @pl.when(k == nk - 1)
    def _():
        rc = min(LN_ROWS, o_ref.shape[0])
        d = o_ref.shape[1]
        cw = min(LN_COLS, d)
        col_blocks = [slice(c, c + cw) for c in range(0, d, cw)]
        inv_d = 1.0 / d

        def lane_fold(v):
            out = v[:, :LANES]
            for t in range(1, cw // LANES):
                out = out + v[:, t * LANES:(t + 1) * LANES]
            return out

        def body(r, carry):
            rows = pl.ds(pl.multiple_of(r * rc, rc), rc)
            acc = jnp.zeros((rc, LANES), F32)
            for cs in col_blocks:
                y = DN_ALPHA * r_ref[rows, cs] + (o_ref[rows, cs] if scale == 1.0 else scale * o_ref[rows, cs])
                o_ref[rows, cs] = y
                acc = acc + lane_fold(y)
            mu = jnp.sum(acc, axis=-1, keepdims=True) * inv_d
            acc = jnp.zeros((rc, LANES), F32)
            for cs in col_blocks:
                yc = o_ref[rows, cs] - mu
                acc = acc + lane_fold(yc * yc)
            rstd = lax.rsqrt(jnp.sum(acc, axis=-1, keepdims=True) * inv_d + LN_EPS)
            for cs in col_blocks:
                out = (o_ref[rows, cs] - mu) * rstd * g_ref[:, cs] + b_ref[:, cs]
                o_ref[rows, cs] = out
                ob_ref[rows, cs] = out.astype(ob_ref.dtype)
            return carry

        lax.fori_loop(0, o_ref.shape[0] // rc, body, 0)


def down_ln(a, w, resid, g, b, scale, w_idx=()):
    t, kdim = a.shape
    d = w.shape[-1]
    tm = _pick(t, (512, 256, 128))
    tk = _pick(kdim, (FFN_K_TILE, 512, 256, 128))
    nk = kdim // tk
    return pl.pallas_call(
        functools.partial(_down_ln_kernel, scale=scale, nk=nk),
        grid=(t // tm, nk),
        in_specs=[pl.BlockSpec((tm, tk), lambda i, k: (i, k)),
                  pl.BlockSpec((None,) * len(w_idx) + (tk, d), lambda i, k: (*w_idx, k, 0)),
                  pl.BlockSpec((tm, d), lambda i, k: (i, 0), pipeline_mode=pl.Buffered(1)),
                  pl.BlockSpec((1, d), lambda i, k: (0, 0)),
                  pl.BlockSpec((1, d), lambda i, k: (0, 0))],
        out_specs=[pl.BlockSpec((tm, d), lambda i, k: (i, 0)),
                   pl.BlockSpec((tm, d), lambda i, k: (i, 0))],
        out_shape=[jax.ShapeDtypeStruct((t, d), F32), jax.ShapeDtypeStruct((t, d), BF16)],
        compiler_params=_cparams("arbitrary", "arbitrary"),
        name="down_ln",
    )(a, w, resid, g.reshape(1, d), b.reshape(1, d))


def _proj_kernel(a_ref, w_ref, o_ref):
    o_ref[...] = jnp.dot(a_ref[...], w_ref[...], preferred_element_type=F32).astype(o_ref.dtype)


def proj(a, w, out_dtype=F32, n=None):
    t, kdim = a.shape
    n = w.shape[1] if n is None else n
    tm = _pick(t, (1024, 512, 256, 128))
    tn = _pick(n, (1024, 768, 512, 384, 256, 128))
    return pl.pallas_call(
        _proj_kernel,
        grid=(t // tm, n // tn),
        in_specs=[pl.BlockSpec((tm, kdim), lambda i, j: (i, 0)),
                  pl.BlockSpec((kdim, tn), lambda i, j: (0, j))],
        out_specs=pl.BlockSpec((tm, tn), lambda i, j: (i, j)),
        out_shape=jax.ShapeDtypeStruct((t, n), out_dtype),
        compiler_params=_cparams("arbitrary", "arbitrary"),
        name="proj",
    )(a, w)


def _rope128(v, c, s):
    return v * c + pltpu.roll(v, 64, axis=1) * s


def _mla_q_kernel(c_ref, gn_ref, w_ref, cos_ref, sin_ref, q_ref, *, heads_per_tile, scale):
    n = _rms_norm(c_ref[...], gn_ref[...]).astype(BF16)
    q = jnp.dot(n, w_ref[...], preferred_element_type=F32) * scale
    c = cos_ref[...]
    s = sin_ref[...]
    for h in range(heads_per_tile):
        o = h * 2 * LANES
        q_ref[:, o:o + LANES] = q[:, o:o + LANES].astype(q_ref.dtype)
        q_ref[:, o + LANES:o + 2 * LANES] = _rope128(q[:, o + LANES:o + 2 * LANES], c, s).astype(q_ref.dtype)


def mla_q_proj(h_main, col_block, rank, gnorm, w, cos_t, sin_t, scale):
    t = h_main.shape[0]
    n = w.shape[1]
    tm = _pick(t, (512, 256, 128))
    tn = _pick(n, (1024, 512, 256))
    return pl.pallas_call(
        functools.partial(_mla_q_kernel, heads_per_tile=tn // (2 * LANES), scale=scale),
        grid=(t // tm, n // tn),
        in_specs=[pl.BlockSpec((tm, rank), lambda i, j: (i, col_block)),
                  pl.BlockSpec((1, rank), lambda i, j: (0, 0)),
                  pl.BlockSpec((rank, tn), lambda i, j: (0, j)),
                  pl.BlockSpec((tm, LANES), lambda i, j: (i, 0)),
                  pl.BlockSpec((tm, LANES), lambda i, j: (i, 0))],
        out_specs=pl.BlockSpec((tm, tn), lambda i, j: (i, j)),
        out_shape=jax.ShapeDtypeStruct((t, n), BF16),
        compiler_params=_cparams("arbitrary", "arbitrary"),
        name="mla_q_proj",
    )(h_main, gnorm.reshape(1, rank), w, cos_t, sin_t)


def _mla_kv_kernel(c_ref, gn_ref, w_ref, o_ref):
    n = _rms_norm(c_ref[...], gn_ref[...]).astype(BF16)
    o_ref[...] = jnp.dot(n, w_ref[...], preferred_element_type=F32).astype(o_ref.dtype)


def mla_kv_proj(h_main, col_block, rank, gnorm, w):
    t = h_main.shape[0]
    n = w.shape[1]
    tm = _pick(t, (512, 256, 128))
    tn = _pick(n, (1024, 512, 256, 128))
    return pl.pallas_call(
        _mla_kv_kernel,
        grid=(t // tm, n // tn),
        in_specs=[pl.BlockSpec((tm, rank), lambda i, j: (i, col_block)),
                  pl.BlockSpec((1, rank), lambda i, j: (0, 0)),
                  pl.BlockSpec((rank, tn), lambda i, j: (0, j))],
        out_specs=pl.BlockSpec((tm, tn), lambda i, j: (i, j)),
        out_shape=jax.ShapeDtypeStruct((t, n), BF16),
        compiler_params=_cparams("arbitrary", "arbitrary"),
        name="mla_kv_proj",
    )(h_main, gnorm.reshape(1, rank), w)


def _ab_small_kernel(hs_ref, cos_ref, sin_ref, wgu_ref, bg_ref, kpe_ref, la_ref):
    hs = hs_ref[...]
    kpe_ref[...] = _rope128(hs[:, :LANES], cos_ref[...], sin_ref[...]).astype(kpe_ref.dtype)
    z = jnp.dot(hs[:, LANES:].astype(BF16), wgu_ref[...], preferred_element_type=F32) + bg_ref[...]
    log_sig = jnp.minimum(z, 0.0) - jnp.log1p(jnp.exp(-jnp.abs(z)))
    la_ref[...] = log_sig / GLA_GATE_NORM


def ab_small(hs, cos_t, sin_t, wgu_pad, b_gate):
    t = hs.shape[0]
    n = wgu_pad.shape[1]
    tm = _pick(t, (512, 256, 128))
    return pl.pallas_call(
        _ab_small_kernel,
        grid=(t // tm,),
        in_specs=[pl.BlockSpec((tm, 2 * LANES), lambda i: (i, 0)),
                  pl.BlockSpec((tm, LANES), lambda i: (i, 0)),
                  pl.BlockSpec((tm, LANES), lambda i: (i, 0)),
                  pl.BlockSpec((LANES, n), lambda i: (0, 0)),
                  pl.BlockSpec((1, n), lambda i: (0, 0))],
        out_specs=[pl.BlockSpec((tm, LANES), lambda i: (i, 0)),
                   pl.BlockSpec((tm, n), lambda i: (i, 0))],
        out_shape=[jax.ShapeDtypeStruct((t, LANES), BF16), jax.ShapeDtypeStruct((t, n), F32)],
        compiler_params=_cparams("arbitrary"),
        name="ab_small",
    )(hs, cos_t, sin_t, wgu_pad, b_gate.reshape(1, n))


ATTN_HEADS = 4


def _mla_attn_kernel(q_ref, kn_ref, kpe_ref, v_ref, o_ref, m_ref, l_ref, acc_ref, *, tq, nh):
    qi = pl.program_id(2)
    heads = range(nh)
    qs = [q_ref[:, h * 2 * LANES:(h + 1) * 2 * LANES] for h in heads]
    m_ref[...] = jnp.full_like(m_ref, -jnp.inf)
    l_ref[...] = jnp.zeros_like(l_ref)
    acc_ref[...] = jnp.zeros_like(acc_ref)

    def step(ki, masked):
        rows = pl.ds(pl.multiple_of(ki * tq, tq), tq)
        kpe = kpe_ref[rows, :]
        ss = [lax.dot_general(qs[h], jnp.concatenate([kn_ref[rows, h * LANES:(h + 1) * LANES], kpe], axis=1),
                              NT, preferred_element_type=F32) for h in heads]
        if masked:
            row = lax.broadcasted_iota(jnp.int32, (tq, tq), 0)
            col = lax.broadcasted_iota(jnp.int32, (tq, tq), 1)
            ss = [jnp.where(col <= row, s, -jnp.inf) for s in ss]
        m_prev = [m_ref[h] for h in heads]
        m_new = [jnp.maximum(m_prev[h], jnp.max(ss[h], axis=-1, keepdims=True)) for h in heads]
        ps = [jnp.exp(ss[h] - jnp.concatenate([m_new[h]] * (tq // LANES), axis=1)) for h in heads]
        alpha = [jnp.exp(m_prev[h] - m_new[h]) for h in heads]
        pv = [jnp.dot(ps[h].astype(BF16), v_ref[rows, h * LANES:(h + 1) * LANES],
                      preferred_element_type=F32) for h in heads]
        for h in heads:
            l_ref[h] = alpha[h] * l_ref[h] + jnp.sum(ps[h], axis=-1, keepdims=True)
            acc_ref[h] = alpha[h] * acc_ref[h] + pv[h]
            m_ref[h] = m_new[h]

    def body(ki, carry):
        step(ki, False)
        return carry

    lax.fori_loop(0, qi, body, 0)
    step(qi, True)
    for h in heads:
        o_ref[:, h * LANES:(h + 1) * LANES] = (acc_ref[h] / l_ref[h]).astype(o_ref.dtype)


def mla_attention(q, kv, kpe, batch, heads):
    t = q.shape[0]
    s = t // batch
    tq = _pick(s, (512, 256, 128))
    nq = s // tq
    nh = _pick(heads, (ATTN_HEADS, 2, 1))
    ng = heads // nh
    return pl.pallas_call(
        functools.partial(_mla_attn_kernel, tq=tq, nh=nh),
        grid=(batch, ng, nq),
        in_specs=[
            pl.BlockSpec((tq, nh * 2 * LANES), lambda b, g, qi: (b * nq + qi, g)),
            pl.BlockSpec((s, nh * LANES), lambda b, g, qi: (b, g)),
            pl.BlockSpec((s, LANES), lambda b, g, qi: (b, 0)),
            pl.BlockSpec((s, nh * LANES), lambda b, g, qi: (b, ng + g)),
        ],
        out_specs=pl.BlockSpec((tq, nh * LANES), lambda b, g, qi: (b * nq + qi, g)),
        out_shape=jax.ShapeDtypeStruct((t, heads * LANES), BF16),
        scratch_shapes=[pltpu.VMEM((nh, tq, LANES), F32), pltpu.VMEM((nh, tq, LANES), F32),
                        pltpu.VMEM((nh, tq, LANES), F32)],
        compiler_params=_cparams("arbitrary", "arbitrary", "arbitrary"),
        name="mla_attention",
    )(q, kv, kpe, kv)


GLA_CHUNKS = 4


def _gla_kernel(q_ref, k_ref, v_ref, la_ref, r_ref, gn_ref, o_ref, st_ref, *, dk, nc):
    c = pl.program_id(2)

    @pl.when(c == 0)
    def _():
        st_ref[...] = jnp.zeros_like(st_ref)

    rows = nc * CHUNK
    ri = lax.broadcasted_iota(jnp.int32, (rows, rows), 0)
    ci = lax.broadcasted_iota(jnp.int32, (rows, rows), 1)
    same_chunk = (ri >> CHUNK_SHIFT) == (ci >> CHUNK_SHIFT)
    tril = jnp.logical_and(same_chunk, ci <= ri)
    la = la_ref[...]
    bcum = _dot_mask(tril.astype(BF16), la)
    b_end = _dot_mask(same_chunk.astype(BF16), la)
    q_in = (q_ref[...] * (dk ** -0.5) * jnp.exp(bcum)).astype(BF16)
    k = k_ref[...]
    v = v_ref[...].astype(BF16)
    k_in = k * jnp.exp(-bcum)
    k_tail = (k * jnp.exp(b_end - bcum)).astype(BF16)
    attn = jnp.where(tril, _dot(q_in, k_in, NT), 0.0)
    o_intra = _dot(attn, v)
    decay = jnp.exp(b_end)
    gn = gn_ref[...]
    st = st_ref[...]
    for j in range(nc):
        sl = slice(j * CHUNK, (j + 1) * CHUNK)
        o = o_intra[sl] + _dot(q_in[sl], st, NT)
        st = st * decay[j * CHUNK:j * CHUNK + 1, :] + _dot(v[sl], k_tail[sl], TN)
        o_ref[sl, :] = (_rms_norm(o, gn) * _silu(r_ref[sl, :])).astype(o_ref.dtype)
    st_ref[...] = st


def gla_mixer(h_main, log_a, gnorm, batch, heads, dk, dv, q_off, k_off, v_off, r_off):
    t = h_main.shape[0]
    nc = _pick(t // batch // CHUNK, (GLA_CHUNKS, 2, 1))
    rows = nc * CHUNK
    n = t // batch // rows
    return pl.pallas_call(
        functools.partial(_gla_kernel, dk=dk, nc=nc),
        grid=(batch, heads, n),
        in_specs=[
            pl.BlockSpec((rows, dk), lambda b, h, c: (b * n + c, q_off // dk + h)),
            pl.BlockSpec((rows, dk), lambda b, h, c: (b * n + c, k_off // dk + h)),
            pl.BlockSpec((rows, dv), lambda b, h, c: (b * n + c, v_off // dv + h)),
            pl.BlockSpec((rows, dk), lambda b, h, c: (b * n + c, h)),
            pl.BlockSpec((rows, dv), lambda b, h, c: (b * n + c, r_off // dv + h)),
            pl.BlockSpec((1, dv), lambda b, h, c: (0, 0)),
        ],
        out_specs=pl.BlockSpec((rows, dv), lambda b, h, c: (b * n + c, h)),
        out_shape=jax.ShapeDtypeStruct((t, heads * dv), BF16),
        scratch_shapes=[pltpu.VMEM((dv, dk), F32)],
        compiler_params=_cparams("arbitrary", "arbitrary", "arbitrary"),
        name="gla_mixer",
    )(h_main, h_main, h_main, log_a, h_main, gnorm.reshape(1, dv))


CONV_ROWS = 512


def _conv_kernel(x_ref, w_ref, o_ref, *, seq, conv_k, n_qk_blocks, q_blocks, q_scale, tc):
    j = pl.program_id(1)
    w = w_ref[...]
    rows = min(CONV_ROWS, seq)

    def conv_chunk(r0, first):
        cur = x_ref[pl.ds(r0, rows), :]
        if first:
            prev = jnp.zeros((8, tc), F32)
        else:
            prev = x_ref[pl.ds(r0 - 8, 8), :]
        ext = jnp.concatenate([prev, cur], axis=0)
        acc = cur * w[conv_k - 1:conv_k, :]
        for d in range(1, conv_k):
            acc = acc + pltpu.roll(ext, d, axis=0)[8:, :] * w[conv_k - 1 - d:conv_k - d, :]
        return _silu(acc)

    def l2n(y):
        parts = []
        for g in range(tc // LANES):
            yg = y[:, g * LANES:(g + 1) * LANES]
            parts.append(yg * lax.rsqrt(jnp.sum(yg * yg, axis=-1, keepdims=True) + L2_EPS))
        return jnp.concatenate(parts, axis=1) if len(parts) > 1 else parts[0]

    for ci in range(seq // rows):
        r0 = ci * rows
        y = conv_chunk(r0, ci == 0)

        @pl.when(j < q_blocks)
        def _():
            o_ref[pl.ds(r0, rows), :] = l2n(y) * q_scale

        @pl.when(jnp.logical_and(j >= q_blocks, j < n_qk_blocks))
        def _():
            o_ref[pl.ds(r0, rows), :] = l2n(y)

        @pl.when(j >= n_qk_blocks)
        def _():
            o_ref[pl.ds(r0, rows), :] = y


def gdn_conv(hc, conv_w, batch, qk_width, conv_ch, q_scale):
    t = hc.shape[0]
    seq = t // batch
    conv_k = conv_w.shape[0]
    tc = 256
    return pl.pallas_call(
        functools.partial(_conv_kernel, seq=seq, conv_k=conv_k, n_qk_blocks=2 * qk_width // tc,
                          q_blocks=qk_width // tc, q_scale=q_scale, tc=tc),
        grid=(batch, conv_ch // tc),
        in_specs=[pl.BlockSpec((seq, tc), lambda b, j: (b, j)),
                  pl.BlockSpec((conv_k, tc), lambda b, j: (0, j))],
        out_specs=pl.BlockSpec((seq, tc), lambda b, j: (b, j)),
        out_shape=jax.ShapeDtypeStruct((t, conv_ch), F32),
        compiler_params=_cparams("arbitrary", "arbitrary"),
        name="gdn_conv",
    )(hc, conv_w)


def _gdn_gates_kernel(hs_ref, negA_ref, dt_ref, o_ref, *, nh):
    hs = hs_ref[...]
    beta = jax.nn.sigmoid(hs)
    z = pltpu.roll(hs, LANES - nh, axis=1) + dt_ref[...]
    sp = jnp.maximum(z, 0.0) + jnp.log1p(jnp.exp(-jnp.abs(z)))
    g = negA_ref[...] * sp
    lane = lax.broadcasted_iota(jnp.int32, hs.shape, 1)
    o_ref[...] = jnp.where(lane < nh, beta, pltpu.roll(g, nh, axis=1))


def gdn_gates(hs, a_log, dt_bias):
    t = hs.shape[0]
    nh = a_log.shape[0]
    tm = _pick(t, (1024, 512, 256, 128))
    pad = LANES - nh
    neg_a = jnp.pad(-jnp.exp(a_log.astype(F32)), (0, pad)).reshape(1, LANES)
    dt = jnp.pad(dt_bias.astype(F32), (0, pad)).reshape(1, LANES)
    return pl.pallas_call(
        functools.partial(_gdn_gates_kernel, nh=nh),
        grid=(t // tm,),
        in_specs=[pl.BlockSpec((tm, LANES), lambda i: (i, 0)),
                  pl.BlockSpec((1, LANES), lambda i: (0, 0)),
                  pl.BlockSpec((1, LANES), lambda i: (0, 0))],
        out_specs=pl.BlockSpec((tm, LANES), lambda i: (i, 0)),
        out_shape=jax.ShapeDtypeStruct((t, LANES), F32),
        compiler_params=_cparams("arbitrary"),
        name="gdn_gates",
    )(hs, neg_a, dt)


GDN_GROUP = 8
INV_BLOCK_SHIFTS = (3, 4, 5)


def _unit_lower_inverses(ms, same_blk, eye):
    m0 = [jnp.where(same_blk[0], m, 0.0) for m in ms]
    m2 = [_dot(x, x) for x in m0]
    a = [eye - x for x in m0]
    t = [_dot(x, y) for x, y in zip(a, m2)]
    m4 = [_dot(x, x) for x in m2]
    a = [x + y for x, y in zip(a, t)]
    t = [_dot(x, y) for x, y in zip(a, m4)]
    a = [x + y for x, y in zip(a, t)]
    inner = same_blk[0]
    for outer in same_blk[1:] + [None]:
        if outer is None:
            sel = jnp.logical_not(inner)
        else:
            sel = jnp.logical_and(outer, jnp.logical_not(inner))
            inner = outer
        ea = [_dot(jnp.where(sel, m, 0.0), x) for m, x in zip(ms, a)]
        t = [_dot(x, y) for x, y in zip(a, ea)]
        a = [x - y for x, y in zip(a, t)]
    return a


def _gdn_kernel(q_ref, k_ref, v_ref, z_ref, gcol_ref, grow_ref, gn_ref, o_ref, st_ref,
                *, nh, dk, dv, rep):
    c = pl.program_id(2)

    @pl.when(c == 0)
    def _():
        st_ref[...] = jnp.zeros_like(st_ref)

    ri = lax.broadcasted_iota(jnp.int32, (CHUNK, CHUNK), 0)
    ci = lax.broadcasted_iota(jnp.int32, (CHUNK, CHUNK), 1)
    incl = ci <= ri
    strict = ci < ri
    gcol = gcol_ref[...]
    grow = grow_ref[0]
    gc_cols = _dot_mask(incl.astype(BF16), gcol)
    gc_rows = _dot_mask((ri <= ci).astype(BF16), grow, mask_first=False)
    gnorm = gn_ref[...]
    same_blk = [(ri >> sh) == (ci >> sh) for sh in INV_BLOCK_SHIFTS]
    eye = (ri == ci).astype(F32)

    heads = range(nh)

    ks = [k_ref[:, kh * dk:(kh + 1) * dk] for kh in range(nh // rep)]
    qs = [q_ref[:, kh * dk:(kh + 1) * dk] for kh in range(nh // rep)]
    kk = [_dot(k, k, NT) for k in ks]
    qk = [_dot(q, k, NT) for q, k in zip(qs, ks)]

    beta = [gcol[:, hh:hh + 1] for hh in heads]
    gc_c = [gc_cols[:, nh + hh:nh + hh + 1] for hh in heads]
    gc_r = [gc_rows[nh + hh:nh + hh + 1, :] for hh in heads]
    gc_last = [g[CHUNK - 1:CHUNK, :] for g in gc_c]
    egc = [jnp.exp(g) for g in gc_c]
    decay = [jnp.exp(jnp.where(incl, gc_c[hh] - gc_r[hh], -jnp.inf)) for hh in heads]
    ms = [jnp.where(strict, beta[hh] * kk[hh // rep] * decay[hh], 0.0) for hh in heads]
    rhs = [jnp.concatenate([v_ref[:, hh * dv:(hh + 1) * dv] * beta[hh],
                            ks[hh // rep] * (beta[hh] * egc[hh])], axis=1) for hh in heads]
    inv = _unit_lower_inverses(ms, same_blk, eye)
    x = [_dot(a, r) for a, r in zip(inv, rhs)]
    st = [st_ref[hh] for hh in heads]
    ws = [_dot(x[hh][:, dv:], st[hh]) for hh in heads]
    o_st = [_dot(qs[hh // rep] * egc[hh], st[hh]) for hh in heads]
    v_new = [x[hh][:, :dv] - ws[hh] for hh in heads]
    o_in = [_dot(qk[hh // rep] * decay[hh], v_new[hh]) for hh in heads]
    kv = [_dot(ks[hh // rep] * jnp.exp(gc_last[hh] - gc_c[hh]), v_new[hh], TN) for hh in heads]
    for hh in heads:
        st_ref[hh] = jnp.exp(gc_last[hh]) * st[hh] + kv[hh]
        zz = z_ref[:, hh * dv:(hh + 1) * dv]
        o = o_st[hh] + o_in[hh]
        o_ref[:, hh * dv:(hh + 1) * dv] = (_rms_norm(o, gnorm) * _silu(zz)).astype(o_ref.dtype)


def gdn_mixer(qkv, hc, gates, gates_t, gnorm, batch, nh, dk, dv, rep, z_off):
    t = qkv.shape[0]
    n = t // batch // CHUNK
    ngrp = nh // GDN_GROUP
    kw = (GDN_GROUP // rep) * dk
    vw = GDN_GROUP * dv
    nkh = nh // rep
    k_blk0 = nkh * dk // kw
    v_blk0 = 2 * nkh * dk // vw
    z_blk0 = z_off // vw
    return pl.pallas_call(
        functools.partial(_gdn_kernel, nh=GDN_GROUP, dk=dk, dv=dv, rep=rep),
        grid=(batch, ngrp, n),
        in_specs=[
            pl.BlockSpec((CHUNK, kw), lambda b, g, c: (b * n + c, g)),
            pl.BlockSpec((CHUNK, kw), lambda b, g, c: (b * n + c, k_blk0 + g)),
            pl.BlockSpec((CHUNK, vw), lambda b, g, c: (b * n + c, v_blk0 + g)),
            pl.BlockSpec((CHUNK, vw), lambda b, g, c: (b * n + c, z_blk0 + g)),
            pl.BlockSpec((CHUNK, LANES), lambda b, g, c: (b * n + c, g)),
            pl.BlockSpec((1, LANES, CHUNK), lambda b, g, c: (b * n + c, g, 0)),
            pl.BlockSpec((1, dv), lambda b, g, c: (0, 0)),
        ],
        out_specs=pl.BlockSpec((CHUNK, vw), lambda b, g, c: (b * n + c, g)),
        out_shape=jax.ShapeDtypeStruct((t, nh * dv), BF16),
        scratch_shapes=[pltpu.VMEM((GDN_GROUP, dk, dv), F32)],
        compiler_params=_cparams("arbitrary", "arbitrary", "arbitrary"),
        name="gdn_mixer",
    )(qkv, qkv, qkv, hc, gates, gates_t, gnorm.reshape(1, dv))


def _xattn_kernel(q_ref, kv_ref, o_ref, *, heads, dim):
    scale = dim ** -0.5
    for h in range(heads):
        q = (q_ref[:, h * dim:(h + 1) * dim] * scale).astype(BF16)
        k = kv_ref[:, h * dim:(h + 1) * dim]
        v = kv_ref[:, (heads + h) * dim:(heads + h + 1) * dim]
        s = lax.dot_general(q, k, NT, preferred_element_type=F32)
        s = s - jnp.max(s, axis=-1, keepdims=True)
        p = jnp.exp(s)
        p = p / jnp.sum(p, axis=-1, keepdims=True)
        o_ref[:, h * dim:(h + 1) * dim] = jnp.dot(p.astype(BF16), v,
                                                  preferred_element_type=F32).astype(o_ref.dtype)


def cross_attention(q, kv, batch, heads, dim):
    t = q.shape[0]
    s = t // batch
    m = kv.shape[0] // batch
    tq = _pick(s, (512, 256, 128))
    nq = s // tq
    return pl.pallas_call(
        functools.partial(_xattn_kernel, heads=heads, dim=dim),
        grid=(batch, nq),
        in_specs=[pl.BlockSpec((tq, heads * dim), lambda b, i: (b * nq + i, 0)),
                  pl.BlockSpec((m, 2 * heads * dim), lambda b, i: (b, 0))],
        out_specs=pl.BlockSpec((tq, heads * dim), lambda b, i: (b * nq + i, 0)),
        out_shape=jax.ShapeDtypeStruct((t, heads * dim), BF16),
        compiler_params=_cparams("arbitrary", "arbitrary"),
        name="cross_attention",
    )(q, kv)


def _rope_cols(w, half):
    x1, x2 = w[:, :half], w[:, half:2 * half]
    return jnp.concatenate([x1, x2, x2, x1], axis=1)


def kernel(x, mem, positions, mem_ln_g, mem_ln_b, ln_g, ln_b, ffn_w_gate, ffn_w_up, ffn_w_down,
           ab_w_in, mla_q_norm, mla_w_uq, mla_kv_norm, mla_w_ukv, gla_w_gate_up, gla_b_gate, gla_norm,
           ab_w_out, c_w_in, gdn_conv_w, gdn_a_log, gdn_dt_bias, gdn_norm, c_w_out,
           xa_wq, xa_wkv, xa_wo):
    batch, seq, d = x.shape
    t = batch * seq
    mem_len = mem.shape[1]
    depth = ln_g.shape[0]

    q_lora = mla_q_norm.shape[1]
    kv_lora = mla_kv_norm.shape[1]
    gla_dv = gla_norm.shape[1]
    gla_dk = gla_dv // 2
    gla_qk = gla_w_gate_up.shape[2]
    gla_heads = gla_qk // gla_dk
    gate_rank = gla_w_gate_up.shape[1]
    mla_heads = (ab_w_out.shape[1] - gla_heads * gla_dv) // LANES
    nope = LANES
    rope = mla_w_uq.shape[2] // mla_heads - nope
    half = rope // 2
    assert rope == 64 and mla_w_ukv.shape[2] == mla_heads * 2 * LANES
    gdn_vh = gdn_a_log.shape[1]
    gdn_dv = gdn_norm.shape[1]
    gdn_dk = gdn_dv
    gdn_vw = gdn_vh * gdn_dv
    gdn_conv_ch = gdn_conv_w.shape[2]
    gdn_qk = (gdn_conv_ch - gdn_vw) // 2
    gdn_kh = gdn_qk // gdn_dk
    xa_dim = LANES
    xa_heads = xa_wq.shape[2] // xa_dim

    inv = 1.0 / (ROPE_THETA ** (jnp.arange(0, rope, 2, dtype=F32) / rope))
    ang = positions.astype(F32).reshape(t, 1) * inv
    cos, sin = jnp.cos(ang), jnp.sin(ang)
    zeros = jnp.zeros((t, 2 * half), F32)
    cos_t = jnp.concatenate([cos, cos, zeros], axis=1)
    sin_t = jnp.concatenate([-sin, sin, zeros], axis=1)

    memn = ln_rows(mem.reshape(batch * mem_len, d), mem_ln_g, mem_ln_b)

    xf = x.reshape(t, d)
    xb = xf.astype(BF16)

    d_ff = ffn_w_gate.shape[-1]
    f_pad = -(-d_ff // FFN_K_TILE) * FFN_K_TILE

    wd_all = cast_pad_rows(ffn_w_down, f_pad)

    def ffn(xf, xb, l, s):
        h = ffn_gateup(xb, ffn_w_gate, ffn_w_up, l, s, f_pad)
        return down_ln(h, wd_all, xf, ln_g[l, 3 * s], ln_b[l, 3 * s], 0.5, w_idx=(l, s))

    for l in range(depth):
        i = l // 2
        xf, xb = ffn(xf, xb, l, 0)

        if l % 2 == 0:
            w_in = ab_w_in[i].astype(BF16)
            o = 0
            segs = []
            for sz in (q_lora, kv_lora, rope, gla_qk, gla_qk, gla_heads * gla_dv, gate_rank,
                       gla_heads * gla_dv):
                segs.append(w_in[:, o:o + sz])
                o += sz
            w_cq, w_ckv, w_kpe, w_gq, w_gk, w_gv, w_glr, w_r = segs
            w_main = jnp.concatenate([w_cq, w_ckv, w_gq, w_gk, w_gv, w_r], axis=1)
            w_small = jnp.concatenate(
                [_rope_cols(w_kpe, half), w_glr, jnp.zeros((d, LANES - gate_rank), BF16)], axis=1)
            h_main = proj(xb, w_main)
            h_small = proj(xb, w_small)
            q_off = q_lora + kv_lora
            k_off = q_off + gla_qk
            v_off = k_off + gla_qk
            r_off = v_off + gla_heads * gla_dv

            wgu_pad = jnp.pad(gla_w_gate_up[i], ((0, LANES - gate_rank), (0, 0))).astype(BF16)
            kpe, log_a = ab_small(h_small, cos_t, sin_t, wgu_pad, gla_b_gate[i])

            wq = mla_w_uq[i].reshape(q_lora, mla_heads, nope + rope)
            wq = jnp.concatenate(
                [wq[:, :, :nope], wq[:, :, nope:nope + half], wq[:, :, nope + half:],
                 wq[:, :, nope + half:], wq[:, :, nope:nope + half]], axis=2)
            wq = wq.reshape(q_lora, mla_heads * 2 * LANES).astype(BF16)
            q = mla_q_proj(h_main, 0, q_lora, mla_q_norm[i], wq, cos_t, sin_t,
                           float((nope + rope) ** -0.5))
            wkv = mla_w_ukv[i].reshape(kv_lora, mla_heads, 2, LANES)
            wkv = jnp.transpose(wkv, (0, 2, 1, 3)).reshape(kv_lora, 2 * mla_heads * LANES).astype(BF16)
            kv = mla_kv_proj(h_main, q_lora // kv_lora, kv_lora, mla_kv_norm[i], wkv)
            o_mla = mla_attention(q, kv, kpe, batch, mla_heads)

            o_gla = gla_mixer(h_main, log_a, gla_norm[i], batch, gla_heads, gla_dk, gla_dv,
                              q_off, k_off, v_off, r_off)
            y_in = jnp.concatenate([o_mla, o_gla], axis=1)
            w_out = ab_w_out[i].astype(BF16)
        else:
            w_in = c_w_in[i].astype(BF16)
            main_w = gdn_conv_ch + gdn_vw
            hc = proj(xb, w_in, n=main_w)
            w_small = jnp.pad(w_in[:, main_w:], ((0, 0), (0, LANES - 2 * gdn_vh)))
            h_small = proj(xb, w_small)
            qkv = gdn_conv(hc, gdn_conv_w[i], batch, gdn_qk, gdn_conv_ch, float(gdn_dk ** -0.5))
            gates = gdn_gates(h_small, gdn_a_log[i], gdn_dt_bias[i])
            ngrp = gdn_vh // GDN_GROUP
            beta_g = gates[:, :gdn_vh].reshape(t, ngrp, GDN_GROUP)
            g_g = gates[:, gdn_vh:2 * gdn_vh].reshape(t, ngrp, GDN_GROUP)
            gg = jnp.concatenate(
                [beta_g, g_g, jnp.zeros((t, ngrp, LANES - 2 * GDN_GROUP), F32)], axis=2)
            gates_c = gg.reshape(t, ngrp * LANES)
            gates_t = jnp.transpose(gg.reshape(t // CHUNK, CHUNK, ngrp * LANES), (0, 2, 1))
            y_in = gdn_mixer(qkv, hc, gates_c, gates_t, gdn_norm[i], batch, gdn_vh, gdn_dk, gdn_dv,
                             gdn_vh // gdn_kh, gdn_conv_ch)
            w_out = c_w_out[i].astype(BF16)

        xf, xb = down_ln(y_in, w_out, xf, ln_g[l, 1], ln_b[l, 1], 1.0)

        qx = proj(xb, xa_wq[l].astype(BF16))
        kvx = proj(memn, xa_wkv[l].astype(BF16), out_dtype=BF16)
        ox = cross_attention(qx, kvx, batch, xa_heads, xa_dim)
        xf, xb = down_ln(ox, xa_wo[l].astype(BF16), xf, ln_g[l, 2], ln_b[l, 2], 1.0)

        xf, xb = ffn(xf, xb, l, 1)

    return xf.reshape(batch, seq, d)
```

```python
import functools

import jax
import jax.numpy as jnp
from jax import lax
from jax.experimental import pallas as pl
from jax.experimental.pallas import tpu as pltpu

F32 = jnp.float32
BF16 = jnp.bfloat16

VMEM_LIMIT_BYTES = 56 * 1024 * 1024
LANES = 128

DEPTH = 2
DN_ALPHA = (2 * DEPTH) ** 0.25
ROPE_THETA = 10000.0
CHUNK = 64
CHUNK_SHIFT = 6
GLA_GATE_NORM = 16.0
LN_EPS = 1e-5
RMS_EPS = 1e-6
L2_EPS = 1e-6
LN_ROWS = 64
LN_COLS = 512
FFN_K_TILE = 1024

NT = (((1,), (1,)), ((), ()))
TN = (((0,), (0,)), ((), ()))


def _cparams(*sem):
    return pltpu.CompilerParams(dimension_semantics=sem, vmem_limit_bytes=VMEM_LIMIT_BYTES)


def _dot(a, b, dims=None):
    a = a.astype(BF16)
    b = b.astype(BF16)
    if dims is None:
        return jnp.dot(a, b, preferred_element_type=F32)
    return lax.dot_general(a, b, dims, preferred_element_type=F32)


def _split3(x):
    hi = x.astype(BF16)
    r = x - hi.astype(F32)
    mid = r.astype(BF16)
    lo = (r - mid.astype(F32)).astype(BF16)
    return hi, mid, lo


def _dot_mask(mask_bf16, x, dims=None, mask_first=True):
    out = None
    for piece in _split3(x):
        if mask_first:
            p = (jnp.dot(mask_bf16, piece, preferred_element_type=F32) if dims is None
                 else lax.dot_general(mask_bf16, piece, dims, preferred_element_type=F32))
        else:
            p = (jnp.dot(piece, mask_bf16, preferred_element_type=F32) if dims is None
                 else lax.dot_general(piece, mask_bf16, dims, preferred_element_type=F32))
        out = p if out is None else out + p
    return out


def _dot_hilo(a, b):
    a_hi = a.astype(BF16)
    a_lo = (a - a_hi.astype(F32)).astype(BF16)
    b_hi = b.astype(BF16)
    b_lo = (b - b_hi.astype(F32)).astype(BF16)
    return (jnp.dot(a_hi, b_hi, preferred_element_type=F32)
            + jnp.dot(a_hi, b_lo, preferred_element_type=F32)
            + jnp.dot(a_lo, b_hi, preferred_element_type=F32))


def _silu(x):
    return x * jax.nn.sigmoid(x)


def _layer_norm(y, g, b):
    mu = jnp.mean(y, axis=-1, keepdims=True)
    yc = y - mu
    var = jnp.mean(yc * yc, axis=-1, keepdims=True)
    return yc * lax.rsqrt(var + LN_EPS) * g + b


def _rms_norm(y, g):
    return y * lax.rsqrt(jnp.mean(y * y, axis=-1, keepdims=True) + RMS_EPS) * g


def _pick(n, prefs):
    for p in prefs:
        if n % p == 0:
            return p
    return n


def _ln_rows_kernel(x_ref, g_ref, b_ref, ob_ref):
    ob_ref[...] = _layer_norm(x_ref[...], g_ref[...], b_ref[...]).astype(ob_ref.dtype)


def ln_rows(x, g, b):
    m, d = x.shape
    tm = _pick(m, (256, 128, 64, 32, 16))
    return pl.pallas_call(
        _ln_rows_kernel,
        grid=(m // tm,),
        in_specs=[pl.BlockSpec((tm, d), lambda i: (i, 0)),
                  pl.BlockSpec((1, d), lambda i: (0, 0)),
                  pl.BlockSpec((1, d), lambda i: (0, 0))],
        out_specs=pl.BlockSpec((tm, d), lambda i: (i, 0)),
        out_shape=jax.ShapeDtypeStruct((m, d), BF16),
        compiler_params=_cparams("arbitrary"),
        name="ln_rows",
    )(x, g.reshape(1, d), b.reshape(1, d))


def _gateup_kernel(x_ref, wg_ref, wu_ref, h_ref, wgb_ref, wub_ref, *, n_real):
    j = pl.program_id(0)

    @pl.when(jnp.logical_and(pl.program_id(1) == 0, j < n_real))
    def _():
        wgb_ref[...] = wg_ref[...].astype(BF16)
        wub_ref[...] = wu_ref[...].astype(BF16)

    @pl.when(j < n_real)
    def _():
        x = x_ref[...]
        g = jnp.dot(x, wgb_ref[...], preferred_element_type=F32)
        u = jnp.dot(x, wub_ref[...], preferred_element_type=F32)
        h_ref[...] = (_silu(g) * u).astype(h_ref.dtype)

    @pl.when(j >= n_real)
    def _():
        h_ref[...] = jnp.zeros_like(h_ref)


def ffn_gateup(xb, wg_all, wu_all, l, s, f_pad):
    t, d = xb.shape
    f = wg_all.shape[-1]
    tm = _pick(t, (1024, 512, 256, 128))
    tn = _pick(f, (256, 128))
    n_real = f // tn
    wspec = pl.BlockSpec((None, None, d, tn), lambda j, i: (l, s, 0, jnp.minimum(j, n_real - 1)))
    return pl.pallas_call(
        functools.partial(_gateup_kernel, n_real=n_real),
        grid=(f_pad // tn, t // tm),
        in_specs=[pl.BlockSpec((tm, d), lambda j, i: (i, 0)), wspec, wspec],
        out_specs=pl.BlockSpec((tm, tn), lambda j, i: (i, j)),
        out_shape=jax.ShapeDtypeStruct((t, f_pad), BF16),
        scratch_shapes=[pltpu.VMEM((d, tn), BF16), pltpu.VMEM((d, tn), BF16)],
        compiler_params=_cparams("arbitrary", "arbitrary"),
        name="ffn_gateup",
    )(xb, wg_all, wu_all)


CAST_ROWS = 256


def _cast_pad_kernel(w_ref, o_ref, *, n_real):
    j = pl.program_id(2)

    @pl.when(j < n_real)
    def _():
        o_ref[...] = w_ref[...].astype(o_ref.dtype)

    @pl.when(j >= n_real)
    def _():
        o_ref[...] = jnp.zeros_like(o_ref)


def cast_pad_rows(w, rows_pad):
    nl, ns, kdim, d = w.shape
    tr = _pick(kdim, (CAST_ROWS, 128))
    n_real = kdim // tr
    return pl.pallas_call(
        functools.partial(_cast_pad_kernel, n_real=n_real),
        grid=(nl, ns, rows_pad // tr),
        in_specs=[pl.BlockSpec((None, None, tr, d), lambda l, s, j: (l, s, jnp.minimum(j, n_real - 1), 0))],
        out_specs=pl.BlockSpec((None, None, tr, d), lambda l, s, j: (l, s, j, 0)),
        out_shape=jax.ShapeDtypeStruct((nl, ns, rows_pad, d), BF16),
        compiler_params=_cparams("arbitrary", "arbitrary", "arbitrary"),
        name="cast_pad_rows",
    )(w)


def _down_ln_kernel(a_ref, w_ref, r_ref, g_ref, b_ref, o_ref, ob_ref, *, scale, nk):
    k = pl.program_id(1)

    if nk == 1:
        o_ref[...] = jnp.dot(a_ref[...], w_ref[...], preferred_element_type=F32)
    else:
        @pl.when(k == 0)
        def _():
            o_ref[...] = jnp.zeros_like(o_ref)

        o_ref[...] += jnp.dot(a_ref[...], w_ref[...], preferred_element_type=F32)

    @pl.when(k == nk - 1)
    def _():
        rc = min(LN_ROWS, o_ref.shape[0])
        d = o_ref.shape[1]
        cw = min(LN_COLS, d)
        col_blocks = [slice(c, c + cw) for c in range(0, d, cw)]
        inv_d = 1.0 / d

        def lane_fold(v):
            out = v[:, :LANES]
            for t in range(1, cw // LANES):
                out = out + v[:, t * LANES:(t + 1) * LANES]
            return out

        def body(r, carry):
            rows = pl.ds(pl.multiple_of(r * rc, rc), rc)
            acc = jnp.zeros((rc, LANES), F32)
            for cs in col_blocks:
                y = DN_ALPHA * r_ref[rows, cs] + (o_ref[rows, cs] if scale == 1.0 else scale * o_ref[rows, cs])
                o_ref[rows, cs] = y
                acc = acc + lane_fold(y)
            mu = jnp.sum(acc, axis=-1, keepdims=True) * inv_d
            acc = jnp.zeros((rc, LANES), F32)
            for cs in col_blocks:
                yc = o_ref[rows, cs] - mu
                acc = acc + lane_fold(yc * yc)
            rstd = lax.rsqrt(jnp.sum(acc, axis=-1, keepdims=True) * inv_d + LN_EPS)
            for cs in col_blocks:
                out = (o_ref[rows, cs] - mu) * rstd * g_ref[:, cs] + b_ref[:, cs]
                o_ref[rows, cs] = out
                ob_ref[rows, cs] = out.astype(ob_ref.dtype)
            return carry

        lax.fori_loop(0, o_ref.shape[0] // rc, body, 0)


def down_ln(a, w, resid, g, b, scale, w_idx=()):
    t, kdim = a.shape
    d = w.shape[-1]
    tm = _pick(t, (512, 256, 128))
    tk = _pick(kdim, (FFN_K_TILE, 512, 256, 128))
    nk = kdim // tk
    r_spec = (pl.BlockSpec((tm, d), lambda i, k: (i, 0)) if nk == 1 else
              pl.BlockSpec((tm, d), lambda i, k: (i, 0), pipeline_mode=pl.Buffered(1)))
    return pl.pallas_call(
        functools.partial(_down_ln_kernel, scale=scale, nk=nk),
        grid=(t // tm, nk),
        in_specs=[pl.BlockSpec((tm, tk), lambda i, k: (i, k)),
                  pl.BlockSpec((None,) * len(w_idx) + (tk, d), lambda i, k: (*w_idx, k, 0)),
                  r_spec,
                  pl.BlockSpec((1, d), lambda i, k: (0, 0)),
                  pl.BlockSpec((1, d), lambda i, k: (0, 0))],
        out_specs=[pl.BlockSpec((tm, d), lambda i, k: (i, 0)),
                   pl.BlockSpec((tm, d), lambda i, k: (i, 0))],
        out_shape=[jax.ShapeDtypeStruct((t, d), F32), jax.ShapeDtypeStruct((t, d), BF16)],
        compiler_params=_cparams("arbitrary", "arbitrary"),
        name="down_ln",
    )(a, w, resid, g.reshape(1, d), b.reshape(1, d))


def _proj_kernel(a_ref, w_ref, o_ref):
    o_ref[...] = jnp.dot(a_ref[...], w_ref[...], preferred_element_type=F32).astype(o_ref.dtype)


def proj(a, w, out_dtype=F32, n=None):
    t, kdim = a.shape
    n = w.shape[1] if n is None else n
    tm = _pick(t, (1024, 512, 256, 128))
    tn = _pick(n, (1024, 768, 512, 384, 256, 128))
    return pl.pallas_call(
        _proj_kernel,
        grid=(t // tm, n // tn),
        in_specs=[pl.BlockSpec((tm, kdim), lambda i, j: (i, 0)),
                  pl.BlockSpec((kdim, tn), lambda i, j: (0, j))],
        out_specs=pl.BlockSpec((tm, tn), lambda i, j: (i, j)),
        out_shape=jax.ShapeDtypeStruct((t, n), out_dtype),
        compiler_params=_cparams("arbitrary", "arbitrary"),
        name="proj",
    )(a, w)


def _rope128(v, c, s):
    return v * c + pltpu.roll(v, 64, axis=1) * s


def _mla_q_kernel(c_ref, gn_ref, w_ref, cos_ref, sin_ref, q_ref, *, heads_per_tile, scale):
    n = _rms_norm(c_ref[...], gn_ref[...]).astype(BF16)
    q = jnp.dot(n, w_ref[...], preferred_element_type=F32) * scale
    c = cos_ref[...]
    s = sin_ref[...]
    for h in range(heads_per_tile):
        o = h * 2 * LANES
        q_ref[:, o:o + LANES] = q[:, o:o + LANES].astype(q_ref.dtype)
        q_ref[:, o + LANES:o + 2 * LANES] = _rope128(q[:, o + LANES:o + 2 * LANES], c, s).astype(q_ref.dtype)


def mla_q_proj(h_main, col_block, rank, gnorm, w, cos_t, sin_t, scale):
    t = h_main.shape[0]
    n = w.shape[1]
    tm = _pick(t, (512, 256, 128))
    tn = _pick(n, (1024, 512, 256))
    return pl.pallas_call(
        functools.partial(_mla_q_kernel, heads_per_tile=tn // (2 * LANES), scale=scale),
        grid=(t // tm, n // tn),
        in_specs=[pl.BlockSpec((tm, rank), lambda i, j: (i, col_block)),
                  pl.BlockSpec((1, rank), lambda i, j: (0, 0)),
                  pl.BlockSpec((rank, tn), lambda i, j: (0, j)),
                  pl.BlockSpec((tm, LANES), lambda i, j: (i, 0)),
                  pl.BlockSpec((tm, LANES), lambda i, j: (i, 0))],
        out_specs=pl.BlockSpec((tm, tn), lambda i, j: (i, j)),
        out_shape=jax.ShapeDtypeStruct((t, n), BF16),
        compiler_params=_cparams("arbitrary", "arbitrary"),
        name="mla_q_proj",
    )(h_main, gnorm.reshape(1, rank), w, cos_t, sin_t)


def _mla_kv_kernel(c_ref, gn_ref, w_ref, o_ref):
    n = _rms_norm(c_ref[...], gn_ref[...]).astype(BF16)
    o_ref[...] = jnp.dot(n, w_ref[...], preferred_element_type=F32).astype(o_ref.dtype)


def mla_kv_proj(h_main, col_block, rank, gnorm, w):
    t = h_main.shape[0]
    n = w.shape[1]
    tm = _pick(t, (512, 256, 128))
    tn = _pick(n, (1024, 512, 256, 128))
    return pl.pallas_call(
        _mla_kv_kernel,
        grid=(t // tm, n // tn),
        in_specs=[pl.BlockSpec((tm, rank), lambda i, j: (i, col_block)),
                  pl.BlockSpec((1, rank), lambda i, j: (0, 0)),
                  pl.BlockSpec((rank, tn), lambda i, j: (0, j))],
        out_specs=pl.BlockSpec((tm, tn), lambda i, j: (i, j)),
        out_shape=jax.ShapeDtypeStruct((t, n), BF16),
        compiler_params=_cparams("arbitrary", "arbitrary"),
        name="mla_kv_proj",
    )(h_main, gnorm.reshape(1, rank), w)


def _ab_small_kernel(hs_ref, cos_ref, sin_ref, wgu_ref, bg_ref, kpe_ref, la_ref):
    hs = hs_ref[...]
    kpe_ref[...] = _rope128(hs[:, :LANES], cos_ref[...], sin_ref[...]).astype(kpe_ref.dtype)
    z = jnp.dot(hs[:, LANES:].astype(BF16), wgu_ref[...], preferred_element_type=F32) + bg_ref[...]
    log_sig = jnp.minimum(z, 0.0) - jnp.log1p(jnp.exp(-jnp.abs(z)))
    la_ref[...] = log_sig / GLA_GATE_NORM


def ab_small(hs, cos_t, sin_t, wgu_pad, b_gate):
    t = hs.shape[0]
    n = wgu_pad.shape[1]
    tm = _pick(t, (512, 256, 128))
    return pl.pallas_call(
        _ab_small_kernel,
        grid=(t // tm,),
        in_specs=[pl.BlockSpec((tm, 2 * LANES), lambda i: (i, 0)),
                  pl.BlockSpec((tm, LANES), lambda i: (i, 0)),
                  pl.BlockSpec((tm, LANES), lambda i: (i, 0)),
                  pl.BlockSpec((LANES, n), lambda i: (0, 0)),
                  pl.BlockSpec((1, n), lambda i: (0, 0))],
        out_specs=[pl.BlockSpec((tm, LANES), lambda i: (i, 0)),
                   pl.BlockSpec((tm, n), lambda i: (i, 0))],
        out_shape=[jax.ShapeDtypeStruct((t, LANES), BF16), jax.ShapeDtypeStruct((t, n), F32)],
        compiler_params=_cparams("arbitrary"),
        name="ab_small",
    )(hs, cos_t, sin_t, wgu_pad, b_gate.reshape(1, n))


ATTN_HEADS = 4


def _mla_attn_kernel(q_ref, kn_ref, kpe_ref, v_ref, o_ref, m_ref, l_ref, acc_ref, *, tq, nh):
    qi = pl.program_id(2)
    heads = range(nh)
    qs = [q_ref[:, h * 2 * LANES:(h + 1) * 2 * LANES] for h in heads]
    m_ref[...] = jnp.full_like(m_ref, -jnp.inf)
    l_ref[...] = jnp.zeros_like(l_ref)
    acc_ref[...] = jnp.zeros_like(acc_ref)

    def step(ki, masked):
        rows = pl.ds(pl.multiple_of(ki * tq, tq), tq)
        kpe = kpe_ref[rows, :]
        ss = [lax.dot_general(qs[h], jnp.concatenate([kn_ref[rows, h * LANES:(h + 1) * LANES], kpe], axis=1),
                              NT, preferred_element_type=F32) for h in heads]
        if masked:
            row = lax.broadcasted_iota(jnp.int32, (tq, tq), 0)
            col = lax.broadcasted_iota(jnp.int32, (tq, tq), 1)
            ss = [jnp.where(col <= row, s, -jnp.inf) for s in ss]
        m_prev = [m_ref[h] for h in heads]
        m_new = [jnp.maximum(m_prev[h], jnp.max(ss[h], axis=-1, keepdims=True)) for h in heads]
        ps = [jnp.exp(ss[h] - jnp.concatenate([m_new[h]] * (tq // LANES), axis=1)) for h in heads]
        alpha = [jnp.exp(m_prev[h] - m_new[h]) for h in heads]
        pv = [jnp.dot(ps[h].astype(BF16), v_ref[rows, h * LANES:(h + 1) * LANES],
                      preferred_element_type=F32) for h in heads]
        for h in heads:
            l_ref[h] = alpha[h] * l_ref[h] + jnp.sum(ps[h], axis=-1, keepdims=True)
            acc_ref[h] = alpha[h] * acc_ref[h] + pv[h]
            m_ref[h] = m_new[h]

    def body(ki, carry):
        step(ki, False)
        return carry

    lax.fori_loop(0, qi, body, 0)
    step(qi, True)
    for h in heads:
        o_ref[:, h * LANES:(h + 1) * LANES] = (acc_ref[h] / l_ref[h]).astype(o_ref.dtype)


def mla_attention(q, kv, kpe, batch, heads):
    t = q.shape[0]
    s = t // batch
    tq = _pick(s, (512, 256, 128))
    nq = s // tq
    nh = _pick(heads, (ATTN_HEADS, 2, 1))
    ng = heads // nh
    return pl.pallas_call(
        functools.partial(_mla_attn_kernel, tq=tq, nh=nh),
        grid=(batch, ng, nq),
        in_specs=[
            pl.BlockSpec((tq, nh * 2 * LANES), lambda b, g, qi: (b * nq + qi, g)),
            pl.BlockSpec((s, nh * LANES), lambda b, g, qi: (b, g)),
            pl.BlockSpec((s, LANES), lambda b, g, qi: (b, 0)),
            pl.BlockSpec((s, nh * LANES), lambda b, g, qi: (b, ng + g)),
        ],
        out_specs=pl.BlockSpec((tq, nh * LANES), lambda b, g, qi: (b * nq + qi, g)),
        out_shape=jax.ShapeDtypeStruct((t, heads * LANES), BF16),
        scratch_shapes=[pltpu.VMEM((nh, tq, LANES), F32), pltpu.VMEM((nh, tq, LANES), F32),
                        pltpu.VMEM((nh, tq, LANES), F32)],
        compiler_params=_cparams("arbitrary", "arbitrary", "arbitrary"),
        name="mla_attention",
    )(q, kv, kpe, kv)


GLA_CHUNKS = 4


def _gla_kernel(q_ref, k_ref, v_ref, la_ref, r_ref, gn_ref, o_ref, st_ref, *, dk, nc):
    c = pl.program_id(2)

    @pl.when(c == 0)
    def _():
        st_ref[...] = jnp.zeros_like(st_ref)

    rows = nc * CHUNK
    ri = lax.broadcasted_iota(jnp.int32, (rows, rows), 0)
    ci = lax.broadcasted_iota(jnp.int32, (rows, rows), 1)
    same_chunk = (ri >> CHUNK_SHIFT) == (ci >> CHUNK_SHIFT)
    tril = jnp.logical_and(same_chunk, ci <= ri)
    la = la_ref[...]
    bcum = _dot_mask(tril.astype(BF16), la)
    b_end = _dot_mask(same_chunk.astype(BF16), la)
    q_in = (q_ref[...] * (dk ** -0.5) * jnp.exp(bcum)).astype(BF16)
    k = k_ref[...]
    v = v_ref[...].astype(BF16)
    k_in = k * jnp.exp(-bcum)
    k_tail = (k * jnp.exp(b_end - bcum)).astype(BF16)
    attn = jnp.where(tril, _dot(q_in, k_in, NT), 0.0)
    o_intra = _dot(attn, v)
    decay = jnp.exp(b_end)
    gn = gn_ref[...]
    st = st_ref[...]
    for j in range(nc):
        sl = slice(j * CHUNK, (j + 1) * CHUNK)
        o = o_intra[sl] + _dot(q_in[sl], st, NT)
        st = st * decay[j * CHUNK:j * CHUNK + 1, :] + _dot(v[sl], k_tail[sl], TN)
        o_ref[sl, :] = (_rms_norm(o, gn) * _silu(r_ref[sl, :])).astype(o_ref.dtype)
    st_ref[...] = st


def gla_mixer(h_main, log_a, gnorm, batch, heads, dk, dv, q_off, k_off, v_off, r_off):
    t = h_main.shape[0]
    nc = _pick(t // batch // CHUNK, (GLA_CHUNKS, 2, 1))
    rows = nc * CHUNK
    n = t // batch // rows
    return pl.pallas_call(
        functools.partial(_gla_kernel, dk=dk, nc=nc),
        grid=(batch, heads, n),
        in_specs=[
            pl.BlockSpec((rows, dk), lambda b, h, c: (b * n + c, q_off // dk + h)),
            pl.BlockSpec((rows, dk), lambda b, h, c: (b * n + c, k_off // dk + h)),
            pl.BlockSpec((rows, dv), lambda b, h, c: (b * n + c, v_off // dv + h)),
            pl.BlockSpec((rows, dk), lambda b, h, c: (b * n + c, h)),
            pl.BlockSpec((rows, dv), lambda b, h, c: (b * n + c, r_off // dv + h)),
            pl.BlockSpec((1, dv), lambda b, h, c: (0, 0)),
        ],
        out_specs=pl.BlockSpec((rows, dv), lambda b, h, c: (b * n + c, h)),
        out_shape=jax.ShapeDtypeStruct((t, heads * dv), BF16),
        scratch_shapes=[pltpu.VMEM((dv, dk), F32)],
        compiler_params=_cparams("arbitrary", "arbitrary", "arbitrary"),
        name="gla_mixer",
    )(h_main, h_main, h_main, log_a, h_main, gnorm.reshape(1, dv))


CONV_ROWS = 512


def _conv_kernel(x_ref, w_ref, o_ref, *, seq, conv_k, n_qk_blocks, q_blocks, q_scale, tc):
    j = pl.program_id(1)
    w = w_ref[...]
    rows = min(CONV_ROWS, seq)

    def conv_chunk(r0, first):
        cur = x_ref[pl.ds(r0, rows), :]
        if first:
            prev = jnp.zeros((8, tc), F32)
        else:
            prev = x_ref[pl.ds(r0 - 8, 8), :]
        ext = jnp.concatenate([prev, cur], axis=0)
        acc = cur * w[conv_k - 1:conv_k, :]
        for d in range(1, conv_k):
            acc = acc + pltpu.roll(ext, d, axis=0)[8:, :] * w[conv_k - 1 - d:conv_k - d, :]
        return _silu(acc)

    def l2n(y):
        parts = []
        for g in range(tc // LANES):
            yg = y[:, g * LANES:(g + 1) * LANES]
            parts.append(yg * lax.rsqrt(jnp.sum(yg * yg, axis=-1, keepdims=True) + L2_EPS))
        return jnp.concatenate(parts, axis=1) if len(parts) > 1 else parts[0]

    for ci in range(seq // rows):
        r0 = ci * rows
        y = conv_chunk(r0, ci == 0)

        @pl.when(j < q_blocks)
        def _():
            o_ref[pl.ds(r0, rows), :] = l2n(y) * q_scale

        @pl.when(jnp.logical_and(j >= q_blocks, j < n_qk_blocks))
        def _():
            o_ref[pl.ds(r0, rows), :] = l2n(y)

        @pl.when(j >= n_qk_blocks)
        def _():
            o_ref[pl.ds(r0, rows), :] = y


def gdn_conv(hc, conv_w, batch, qk_width, conv_ch, q_scale):
    t = hc.shape[0]
    seq = t // batch
    conv_k = conv_w.shape[0]
    tc = 256
    return pl.pallas_call(
        functools.partial(_conv_kernel, seq=seq, conv_k=conv_k, n_qk_blocks=2 * qk_width // tc,
                          q_blocks=qk_width // tc, q_scale=q_scale, tc=tc),
        grid=(batch, conv_ch // tc),
        in_specs=[pl.BlockSpec((seq, tc), lambda b, j: (b, j)),
                  pl.BlockSpec((conv_k, tc), lambda b, j: (0, j))],
        out_specs=pl.BlockSpec((seq, tc), lambda b, j: (b, j)),
        out_shape=jax.ShapeDtypeStruct((t, conv_ch), F32),
        compiler_params=_cparams("arbitrary", "arbitrary"),
        name="gdn_conv",
    )(hc, conv_w)


def _gdn_gates_kernel(hs_ref, negA_ref, dt_ref, o_ref, *, nh):
    hs = hs_ref[...]
    beta = jax.nn.sigmoid(hs)
    z = pltpu.roll(hs, LANES - nh, axis=1) + dt_ref[...]
    sp = jnp.maximum(z, 0.0) + jnp.log1p(jnp.exp(-jnp.abs(z)))
    g = negA_ref[...] * sp
    lane = lax.broadcasted_iota(jnp.int32, hs.shape, 1)
    o_ref[...] = jnp.where(lane < nh, beta, pltpu.roll(g, nh, axis=1))


def gdn_gates(hs, a_log, dt_bias):
    t = hs.shape[0]
    nh = a_log.shape[0]
    tm = _pick(t, (1024, 512, 256, 128))
    pad = LANES - nh
    neg_a = jnp.pad(-jnp.exp(a_log.astype(F32)), (0, pad)).reshape(1, LANES)
    dt = jnp.pad(dt_bias.astype(F32), (0, pad)).reshape(1, LANES)
    return pl.pallas_call(
        functools.partial(_gdn_gates_kernel, nh=nh),
        grid=(t // tm,),
        in_specs=[pl.BlockSpec((tm, LANES), lambda i: (i, 0)),
                  pl.BlockSpec((1, LANES), lambda i: (0, 0)),
                  pl.BlockSpec((1, LANES), lambda i: (0, 0))],
        out_specs=pl.BlockSpec((tm, LANES), lambda i: (i, 0)),
        out_shape=jax.ShapeDtypeStruct((t, LANES), F32),
        compiler_params=_cparams("arbitrary"),
        name="gdn_gates",
    )(hs, neg_a, dt)


GDN_GROUP = 32
INV_BLOCK_SHIFTS = (3, 4, 5)


def _unit_lower_inverses(ms, same_blk, eye):
    m0 = [jnp.where(same_blk[0], m, 0.0) for m in ms]
    m2 = [_dot(x, x) for x in m0]
    a = [eye - x for x in m0]
    t = [_dot(x, y) for x, y in zip(a, m2)]
    m4 = [_dot(x, x) for x in m2]
    a = [x + y for x, y in zip(a, t)]
    t = [_dot(x, y) for x, y in zip(a, m4)]
    a = [x + y for x, y in zip(a, t)]
    inner = same_blk[0]
    for outer in same_blk[1:] + [None]:
        if outer is None:
            sel = jnp.logical_not(inner)
        else:
            sel = jnp.logical_and(outer, jnp.logical_not(inner))
            inner = outer
        ea = [_dot(jnp.where(sel, m, 0.0), x) for m, x in zip(ms, a)]
        t = [_dot(x, y) for x, y in zip(a, ea)]
        a = [x - y for x, y in zip(a, t)]
    return a


def _gdn_kernel(q_ref, k_ref, v_ref, z_ref, gcol_ref, grow_ref, gn_ref, o_ref, st_ref,
                *, nh, dk, dv, rep):
    c = pl.program_id(2)

    @pl.when(c == 0)
    def _():
        st_ref[...] = jnp.zeros_like(st_ref)

    ri = lax.broadcasted_iota(jnp.int32, (CHUNK, CHUNK), 0)
    ci = lax.broadcasted_iota(jnp.int32, (CHUNK, CHUNK), 1)
    incl = ci <= ri
    strict = ci < ri
    gcol = gcol_ref[...]
    grow = grow_ref[0]
    gc_cols = _dot_mask(incl.astype(BF16), gcol)
    gc_rows = _dot_mask((ri <= ci).astype(BF16), grow, mask_first=False)
    gnorm = gn_ref[...]
    same_blk = [(ri >> sh) == (ci >> sh) for sh in INV_BLOCK_SHIFTS]
    eye = (ri == ci).astype(F32)

    heads = range(nh)

    ks = [k_ref[:, kh * dk:(kh + 1) * dk] for kh in range(nh // rep)]
    qs = [q_ref[:, kh * dk:(kh + 1) * dk] for kh in range(nh // rep)]
    kk = [_dot(k, k, NT) for k in ks]
    qk = [_dot(q, k, NT) for q, k in zip(qs, ks)]

    beta = [gcol[:, hh:hh + 1] for hh in heads]
    gc_c = [gc_cols[:, nh + hh:nh + hh + 1] for hh in heads]
    gc_r = [gc_rows[nh + hh:nh + hh + 1, :] for hh in heads]
    gc_last = [g[CHUNK - 1:CHUNK, :] for g in gc_c]
    egc = [jnp.exp(g) for g in gc_c]
    decay = [jnp.exp(jnp.where(incl, gc_c[hh] - gc_r[hh], -jnp.inf)) for hh in heads]
    ms = [jnp.where(strict, beta[hh] * kk[hh // rep] * decay[hh], 0.0) for hh in heads]
    rhs = [jnp.concatenate([v_ref[:, hh * dv:(hh + 1) * dv] * beta[hh],
                            ks[hh // rep] * (beta[hh] * egc[hh])], axis=1) for hh in heads]
    inv = _unit_lower_inverses(ms, same_blk, eye)
    x = [_dot(a, r) for a, r in zip(inv, rhs)]
    st = [st_ref[hh] for hh in heads]
    ws = [_dot(x[hh][:, dv:], st[hh]) for hh in heads]
    o_st = [_dot(qs[hh // rep] * egc[hh], st[hh]) for hh in heads]
    v_new = [x[hh][:, :dv] - ws[hh] for hh in heads]
    o_in = [_dot(qk[hh // rep] * decay[hh], v_new[hh]) for hh in heads]
    kv = [_dot(ks[hh // rep] * jnp.exp(gc_last[hh] - gc_c[hh]), v_new[hh], TN) for hh in heads]
    for hh in heads:
        st_ref[hh] = jnp.exp(gc_last[hh]) * st[hh] + kv[hh]
        zz = z_ref[:, hh * dv:(hh + 1) * dv]
        o = o_st[hh] + o_in[hh]
        o_ref[:, hh * dv:(hh + 1) * dv] = (_rms_norm(o, gnorm) * _silu(zz)).astype(o_ref.dtype)


def gdn_mixer(qkv, hc, gates, gates_t, gnorm, batch, nh, dk, dv, rep, z_off):
    t = qkv.shape[0]
    n = t // batch // CHUNK
    grp = min(GDN_GROUP, nh)
    ngrp = nh // grp
    kw = (grp // rep) * dk
    vw = grp * dv
    nkh = nh // rep
    k_blk0 = nkh * dk // kw
    v_blk0 = 2 * nkh * dk // vw
    z_blk0 = z_off // vw
    return pl.pallas_call(
        functools.partial(_gdn_kernel, nh=grp, dk=dk, dv=dv, rep=rep),
        grid=(batch, ngrp, n),
        in_specs=[
            pl.BlockSpec((CHUNK, kw), lambda b, g, c: (b * n + c, g)),
            pl.BlockSpec((CHUNK, kw), lambda b, g, c: (b * n + c, k_blk0 + g)),
            pl.BlockSpec((CHUNK, vw), lambda b, g, c: (b * n + c, v_blk0 + g)),
            pl.BlockSpec((CHUNK, vw), lambda b, g, c: (b * n + c, z_blk0 + g)),
            pl.BlockSpec((CHUNK, LANES), lambda b, g, c: (b * n + c, g)),
            pl.BlockSpec((1, LANES, CHUNK), lambda b, g, c: (b * n + c, g, 0)),
            pl.BlockSpec((1, dv), lambda b, g, c: (0, 0)),
        ],
        out_specs=pl.BlockSpec((CHUNK, vw), lambda b, g, c: (b * n + c, g)),
        out_shape=jax.ShapeDtypeStruct((t, nh * dv), BF16),
        scratch_shapes=[pltpu.VMEM((grp, dk, dv), F32)],
        compiler_params=_cparams("arbitrary", "arbitrary", "arbitrary"),
        name="gdn_mixer",
    )(qkv, qkv, qkv, hc, gates, gates_t, gnorm.reshape(1, dv))


def _xattn_kernel(q_ref, kv_ref, o_ref, *, heads, dim):
    scale = dim ** -0.5
    for h in range(heads):
        q = (q_ref[:, h * dim:(h + 1) * dim] * scale).astype(BF16)
        k = kv_ref[:, h * dim:(h + 1) * dim]
        v = kv_ref[:, (heads + h) * dim:(heads + h + 1) * dim]
        s = lax.dot_general(q, k, NT, preferred_element_type=F32)
        s = s - jnp.max(s, axis=-1, keepdims=True)
        p = jnp.exp(s)
        p = p / jnp.sum(p, axis=-1, keepdims=True)
        o_ref[:, h * dim:(h + 1) * dim] = jnp.dot(p.astype(BF16), v,
                                                  preferred_element_type=F32).astype(o_ref.dtype)


def cross_attention(q, kv, batch, heads, dim):
    t = q.shape[0]
    s = t // batch
    m = kv.shape[0] // batch
    tq = _pick(s, (512, 256, 128))
    nq = s // tq
    return pl.pallas_call(
        functools.partial(_xattn_kernel, heads=heads, dim=dim),
        grid=(batch, nq),
        in_specs=[pl.BlockSpec((tq, heads * dim), lambda b, i: (b * nq + i, 0)),
                  pl.BlockSpec((m, 2 * heads * dim), lambda b, i: (b, 0))],
        out_specs=pl.BlockSpec((tq, heads * dim), lambda b, i: (b * nq + i, 0)),
        out_shape=jax.ShapeDtypeStruct((t, heads * dim), BF16),
        compiler_params=_cparams("arbitrary", "arbitrary"),
        name="cross_attention",
    )(q, kv)


def _rope_cols(w, half):
    x1, x2 = w[:, :half], w[:, half:2 * half]
    return jnp.concatenate([x1, x2, x2, x1], axis=1)


def kernel(x, mem, positions, mem_ln_g, mem_ln_b, ln_g, ln_b, ffn_w_gate, ffn_w_up, ffn_w_down,
           ab_w_in, mla_q_norm, mla_w_uq, mla_kv_norm, mla_w_ukv, gla_w_gate_up, gla_b_gate, gla_norm,
           ab_w_out, c_w_in, gdn_conv_w, gdn_a_log, gdn_dt_bias, gdn_norm, c_w_out,
           xa_wq, xa_wkv, xa_wo):
    batch, seq, d = x.shape
    t = batch * seq
    mem_len = mem.shape[1]
    depth = ln_g.shape[0]

    q_lora = mla_q_norm.shape[1]
    kv_lora = mla_kv_norm.shape[1]
    gla_dv = gla_norm.shape[1]
    gla_dk = gla_dv // 2
    gla_qk = gla_w_gate_up.shape[2]
    gla_heads = gla_qk // gla_dk
    gate_rank = gla_w_gate_up.shape[1]
    mla_heads = (ab_w_out.shape[1] - gla_heads * gla_dv) // LANES
    nope = LANES
    rope = mla_w_uq.shape[2] // mla_heads - nope
    half = rope // 2
    assert rope == 64 and mla_w_ukv.shape[2] == mla_heads * 2 * LANES
    gdn_vh = gdn_a_log.shape[1]
    gdn_dv = gdn_norm.shape[1]
    gdn_dk = gdn_dv
    gdn_vw = gdn_vh * gdn_dv
    gdn_conv_ch = gdn_conv_w.shape[2]
    gdn_qk = (gdn_conv_ch - gdn_vw) // 2
    gdn_kh = gdn_qk // gdn_dk
    xa_dim = LANES
    xa_heads = xa_wq.shape[2] // xa_dim

    inv = 1.0 / (ROPE_THETA ** (jnp.arange(0, rope, 2, dtype=F32) / rope))
    ang = positions.astype(F32).reshape(t, 1) * inv
    cos, sin = jnp.cos(ang), jnp.sin(ang)
    zeros = jnp.zeros((t, 2 * half), F32)
    cos_t = jnp.concatenate([cos, cos, zeros], axis=1)
    sin_t = jnp.concatenate([-sin, sin, zeros], axis=1)

    memn = ln_rows(mem.reshape(batch * mem_len, d), mem_ln_g, mem_ln_b)

    xf = x.reshape(t, d)
    xb = xf.astype(BF16)

    d_ff = ffn_w_gate.shape[-1]
    f_pad = -(-d_ff // FFN_K_TILE) * FFN_K_TILE

    wd_all = cast_pad_rows(ffn_w_down, f_pad)

    def ffn(xf, xb, l, s):
        h = ffn_gateup(xb, ffn_w_gate, ffn_w_up, l, s, f_pad)
        return down_ln(h, wd_all, xf, ln_g[l, 3 * s], ln_b[l, 3 * s], 0.5, w_idx=(l, s))

    for l in range(depth):
        i = l // 2
        xf, xb = ffn(xf, xb, l, 0)

        if l % 2 == 0:
            w_in = ab_w_in[i].astype(BF16)
            o = 0
            segs = []
            for sz in (q_lora, kv_lora, rope, gla_qk, gla_qk, gla_heads * gla_dv, gate_rank,
                       gla_heads * gla_dv):
                segs.append(w_in[:, o:o + sz])
                o += sz
            w_cq, w_ckv, w_kpe, w_gq, w_gk, w_gv, w_glr, w_r = segs
            w_main = jnp.concatenate([w_cq, w_ckv, w_gq, w_gk, w_gv, w_r], axis=1)
            w_small = jnp.concatenate(
                [_rope_cols(w_kpe, half), w_glr, jnp.zeros((d, LANES - gate_rank), BF16)], axis=1)
            h_main = proj(xb, w_main)
            h_small = proj(xb, w_small)
            q_off = q_lora + kv_lora
            k_off = q_off + gla_qk
            v_off = k_off + gla_qk
            r_off = v_off + gla_heads * gla_dv

            wgu_pad = jnp.pad(gla_w_gate_up[i], ((0, LANES - gate_rank), (0, 0))).astype(BF16)
            kpe, log_a = ab_small(h_small, cos_t, sin_t, wgu_pad, gla_b_gate[i])

            wq = mla_w_uq[i].reshape(q_lora, mla_heads, nope + rope)
            wq = jnp.concatenate(
                [wq[:, :, :nope], wq[:, :, nope:nope + half], wq[:, :, nope + half:],
                 wq[:, :, nope + half:], wq[:, :, nope:nope + half]], axis=2)
            wq = wq.reshape(q_lora, mla_heads * 2 * LANES).astype(BF16)
            q = mla_q_proj(h_main, 0, q_lora, mla_q_norm[i], wq, cos_t, sin_t,
                           float((nope + rope) ** -0.5))
            wkv = mla_w_ukv[i].reshape(kv_lora, mla_heads, 2, LANES)
            wkv = jnp.transpose(wkv, (0, 2, 1, 3)).reshape(kv_lora, 2 * mla_heads * LANES).astype(BF16)
            kv = mla_kv_proj(h_main, q_lora // kv_lora, kv_lora, mla_kv_norm[i], wkv)
            o_mla = mla_attention(q, kv, kpe, batch, mla_heads)

            o_gla = gla_mixer(h_main, log_a, gla_norm[i], batch, gla_heads, gla_dk, gla_dv,
                              q_off, k_off, v_off, r_off)
            y_in = jnp.concatenate([o_mla, o_gla], axis=1)
            w_out = ab_w_out[i].astype(BF16)
        else:
            w_in = c_w_in[i].astype(BF16)
            main_w = gdn_conv_ch + gdn_vw
            hc = proj(xb, w_in, n=main_w)
            w_small = jnp.pad(w_in[:, main_w:], ((0, 0), (0, LANES - 2 * gdn_vh)))
            h_small = proj(xb, w_small)
            qkv = gdn_conv(hc, gdn_conv_w[i], batch, gdn_qk, gdn_conv_ch, float(gdn_dk ** -0.5))
            gates = gdn_gates(h_small, gdn_a_log[i], gdn_dt_bias[i])
            grp = min(GDN_GROUP, gdn_vh)
            ngrp = gdn_vh // grp
            beta_g = gates[:, :gdn_vh].reshape(t, ngrp, grp)
            g_g = gates[:, gdn_vh:2 * gdn_vh].reshape(t, ngrp, grp)
            gg = jnp.concatenate(
                [beta_g, g_g, jnp.zeros((t, ngrp, LANES - 2 * grp), F32)], axis=2)
            gates_c = gg.reshape(t, ngrp * LANES)
            gates_t = jnp.transpose(gg.reshape(t // CHUNK, CHUNK, ngrp * LANES), (0, 2, 1))
            y_in = gdn_mixer(qkv, hc, gates_c, gates_t, gdn_norm[i], batch, gdn_vh, gdn_dk, gdn_dv,
                             gdn_vh // gdn_kh, gdn_conv_ch)
            w_out = c_w_out[i].astype(BF16)

        xf, xb = down_ln(y_in, w_out, xf, ln_g[l, 1], ln_b[l, 1], 1.0)

        qx = proj(xb, xa_wq[l].astype(BF16))
        kvx = proj(memn, xa_wkv[l].astype(BF16), out_dtype=BF16)
        ox = cross_attention(qx, kvx, batch, xa_heads, xa_dim)
        xf, xb = down_ln(ox, xa_wo[l].astype(BF16), xf, ln_g[l, 2], ln_b[l, 2], 1.0)

        xf, xb = ffn(xf, xb, l, 1)

    return xf.reshape(batch, seq, d)
```

```python
import functools

import jax
import jax.numpy as jnp
from jax import lax
from jax.experimental import pallas as pl
from jax.experimental.pallas import tpu as pltpu

F32 = jnp.float32
BF16 = jnp.bfloat16

VMEM_LIMIT_BYTES = 56 * 1024 * 1024
LANES = 128

DEPTH = 2
DN_ALPHA = (2 * DEPTH) ** 0.25
ROPE_THETA = 10000.0
CHUNK = 64
CHUNK_SHIFT = 6
GLA_GATE_NORM = 16.0
LN_EPS = 1e-5
RMS_EPS = 1e-6
L2_EPS = 1e-6
LN_ROWS = 64
LN_COLS = 512
FFN_K_TILE = 1024

NT = (((1,), (1,)), ((), ()))
TN = (((0,), (0,)), ((), ()))


def _cparams(*sem):
    return pltpu.CompilerParams(dimension_semantics=sem, vmem_limit_bytes=VMEM_LIMIT_BYTES)


def _dot(a, b, dims=None):
    a = a.astype(BF16)
    b = b.astype(BF16)
    if dims is None:
        return jnp.dot(a, b, preferred_element_type=F32)
    return lax.dot_general(a, b, dims, preferred_element_type=F32)


def _split3(x):
    hi = x.astype(BF16)
    r = x - hi.astype(F32)
    mid = r.astype(BF16)
    lo = (r - mid.astype(F32)).astype(BF16)
    return hi, mid, lo


def _dot_mask(mask_bf16, x, dims=None, mask_first=True):
    out = None
    for piece in _split3(x):
        if mask_first:
            p = (jnp.dot(mask_bf16, piece, preferred_element_type=F32) if dims is None
                 else lax.dot_general(mask_bf16, piece, dims, preferred_element_type=F32))
        else:
            p = (jnp.dot(piece, mask_bf16, preferred_element_type=F32) if dims is None
                 else lax.dot_general(piece, mask_bf16, dims, preferred_element_type=F32))
        out = p if out is None else out + p
    return out


def _dot_hilo(a, b):
    a_hi = a.astype(BF16)
    a_lo = (a - a_hi.astype(F32)).astype(BF16)
    b_hi = b.astype(BF16)
    b_lo = (b - b_hi.astype(F32)).astype(BF16)
    return (jnp.dot(a_hi, b_hi, preferred_element_type=F32)
            + jnp.dot(a_hi, b_lo, preferred_element_type=F32)
            + jnp.dot(a_lo, b_hi, preferred_element_type=F32))


def _silu(x):
    return x * jax.nn.sigmoid(x)


def _layer_norm(y, g, b):
    mu = jnp.mean(y, axis=-1, keepdims=True)
    yc = y - mu
    var = jnp.mean(yc * yc, axis=-1, keepdims=True)
    return yc * lax.rsqrt(var + LN_EPS) * g + b


def _rms_norm(y, g):
    return y * lax.rsqrt(jnp.mean(y * y, axis=-1, keepdims=True) + RMS_EPS) * g


def _pick(n, prefs):
    for p in prefs:
        if n % p == 0:
            return p
    return n


def _ln_rows_kernel(x_ref, g_ref, b_ref, ob_ref):
    ob_ref[...] = _layer_norm(x_ref[...], g_ref[...], b_ref[...]).astype(ob_ref.dtype)


def ln_rows(x, g, b):
    m, d = x.shape
    tm = _pick(m, (256, 128, 64, 32, 16))
    return pl.pallas_call(
        _ln_rows_kernel,
        grid=(m // tm,),
        in_specs=[pl.BlockSpec((tm, d), lambda i: (i, 0)),
                  pl.BlockSpec((1, d), lambda i: (0, 0)),
                  pl.BlockSpec((1, d), lambda i: (0, 0))],
        out_specs=pl.BlockSpec((tm, d), lambda i: (i, 0)),
        out_shape=jax.ShapeDtypeStruct((m, d), BF16),
        compiler_params=_cparams("arbitrary"),
        name="ln_rows",
    )(x, g.reshape(1, d), b.reshape(1, d))


def _gateup_kernel(x_ref, wg_ref, wu_ref, h_ref, wgb_ref, wub_ref, *, n_real):
    j = pl.program_id(0)

    @pl.when(jnp.logical_and(pl.program_id(1) == 0, j < n_real))
    def _():
        wgb_ref[...] = wg_ref[...].astype(BF16)
        wub_ref[...] = wu_ref[...].astype(BF16)

    @pl.when(j < n_real)
    def _():
        x = x_ref[...]
        g = jnp.dot(x, wgb_ref[...], preferred_element_type=F32)
        u = jnp.dot(x, wub_ref[...], preferred_element_type=F32)
        h_ref[...] = (_silu(g) * u).astype(h_ref.dtype)

    @pl.when(j >= n_real)
    def _():
        h_ref[...] = jnp.zeros_like(h_ref)


def ffn_gateup(xb, wg_all, wu_all, l, s, f_pad):
    t, d = xb.shape
    f = wg_all.shape[-1]
    tm = _pick(t, (1024, 512, 256, 128))
    tn = _pick(f, (256, 128))
    n_real = f // tn
    wspec = pl.BlockSpec((None, None, d, tn), lambda j, i: (l, s, 0, jnp.minimum(j, n_real - 1)))
    return pl.pallas_call(
        functools.partial(_gateup_kernel, n_real=n_real),
        grid=(f_pad // tn, t // tm),
        in_specs=[pl.BlockSpec((tm, d), lambda j, i: (i, 0)), wspec, wspec],
        out_specs=pl.BlockSpec((tm, tn), lambda j, i: (i, j)),
        out_shape=jax.ShapeDtypeStruct((t, f_pad), BF16),
        scratch_shapes=[pltpu.VMEM((d, tn), BF16), pltpu.VMEM((d, tn), BF16)],
        compiler_params=_cparams("arbitrary", "arbitrary"),
        name="ffn_gateup",
    )(xb, wg_all, wu_all)


CAST_ROWS = 256


def _cast_pad_kernel(w_ref, o_ref, *, n_real):
    j = pl.program_id(2)

    @pl.when(j < n_real)
    def _():
        o_ref[...] = w_ref[...].astype(o_ref.dtype)

    @pl.when(j >= n_real)
    def _():
        o_ref[...] = jnp.zeros_like(o_ref)


def cast_pad_rows(w, rows_pad):
    nl, ns, kdim, d = w.shape
    tr = _pick(kdim, (CAST_ROWS, 128))
    n_real = kdim // tr
    return pl.pallas_call(
        functools.partial(_cast_pad_kernel, n_real=n_real),
        grid=(nl, ns, rows_pad // tr),
        in_specs=[pl.BlockSpec((None, None, tr, d), lambda l, s, j: (l, s, jnp.minimum(j, n_real - 1), 0))],
        out_specs=pl.BlockSpec((None, None, tr, d), lambda l, s, j: (l, s, j, 0)),
        out_shape=jax.ShapeDtypeStruct((nl, ns, rows_pad, d), BF16),
        compiler_params=_cparams("arbitrary", "arbitrary", "arbitrary"),
        name="cast_pad_rows",
    )(w)


def _down_ln_kernel(a_ref, w_ref, r_ref, g_ref, b_ref, o_ref, ob_ref, *scratch, scale, nk, nrc):
    k = pl.program_id(1)

    if nk == 1:
        res_ref = r_ref
        o_ref[...] = jnp.dot(a_ref[...], w_ref[...], preferred_element_type=F32)
    else:
        (res_ref,) = scratch
        cwid = res_ref.shape[1] // nrc
        for c in range(nrc):
            @pl.when(k == c)
            def _(c=c):
                res_ref[:, c * cwid:(c + 1) * cwid] = r_ref[...]

        @pl.when(k == 0)
        def _():
            o_ref[...] = jnp.zeros_like(o_ref)

        o_ref[...] += jnp.dot(a_ref[...], w_ref[...], preferred_element_type=F32)

    @pl.when(k == nk - 1)
    def _():
        rc = min(LN_ROWS, o_ref.shape[0])
        d = o_ref.shape[1]
        cw = min(LN_COLS, d)
        col_blocks = [slice(c, c + cw) for c in range(0, d, cw)]
        inv_d = 1.0 / d

        def lane_fold(v):
            out = v[:, :LANES]
            for t in range(1, cw // LANES):
                out = out + v[:, t * LANES:(t + 1) * LANES]
            return out

        def body(r, carry):
            rows = pl.ds(pl.multiple_of(r * rc, rc), rc)
            acc = jnp.zeros((rc, LANES), F32)
            for cs in col_blocks:
                y = DN_ALPHA * res_ref[rows, cs] + (o_ref[rows, cs] if scale == 1.0 else scale * o_ref[rows, cs])
                o_ref[rows, cs] = y
                acc = acc + lane_fold(y)
            mu = jnp.sum(acc, axis=-1, keepdims=True) * inv_d
            acc = jnp.zeros((rc, LANES), F32)
            for cs in col_blocks:
                yc = o_ref[rows, cs] - mu
                acc = acc + lane_fold(yc * yc)
            rstd = lax.rsqrt(jnp.sum(acc, axis=-1, keepdims=True) * inv_d + LN_EPS)
            for cs in col_blocks:
                out = (o_ref[rows, cs] - mu) * rstd * g_ref[:, cs] + b_ref[:, cs]
                o_ref[rows, cs] = out
                ob_ref[rows, cs] = out.astype(ob_ref.dtype)
            return carry

        lax.fori_loop(0, o_ref.shape[0] // rc, body, 0)


def down_ln(a, w, resid, g, b, scale, w_idx=()):
    t, kdim = a.shape
    d = w.shape[-1]
    tm = _pick(t, (512, 256, 128))
    tk = _pick(kdim, (FFN_K_TILE, 512, 256, 128))
    nk = kdim // tk
    nrc = 1
    while nrc * 2 <= nk and d % (nrc * 2 * LANES) == 0:
        nrc *= 2
    if nk == 1:
        r_spec = pl.BlockSpec((tm, d), lambda i, k: (i, 0))
        scratch = []
    else:
        r_spec = pl.BlockSpec((tm, d // nrc), lambda i, k: (i, jnp.minimum(k, nrc - 1)))
        scratch = [pltpu.VMEM((tm, d), F32)]
    return pl.pallas_call(
        functools.partial(_down_ln_kernel, scale=scale, nk=nk, nrc=nrc),
        grid=(t // tm, nk),
        in_specs=[pl.BlockSpec((tm, tk), lambda i, k: (i, k)),
                  pl.BlockSpec((None,) * len(w_idx) + (tk, d), lambda i, k: (*w_idx, k, 0)),
                  r_spec,
                  pl.BlockSpec((1, d), lambda i, k: (0, 0)),
                  pl.BlockSpec((1, d), lambda i, k: (0, 0))],
        out_specs=[pl.BlockSpec((tm, d), lambda i, k: (i, 0)),
                   pl.BlockSpec((tm, d), lambda i, k: (i, 0))],
        out_shape=[jax.ShapeDtypeStruct((t, d), F32), jax.ShapeDtypeStruct((t, d), BF16)],
        scratch_shapes=scratch,
        compiler_params=_cparams("arbitrary", "arbitrary"),
        name="down_ln",
    )(a, w, resid, g.reshape(1, d), b.reshape(1, d))


def _proj_kernel(a_ref, w_ref, o_ref):
    o_ref[...] = jnp.dot(a_ref[...], w_ref[...], preferred_element_type=F32).astype(o_ref.dtype)


def proj(a, w, out_dtype=F32, n=None):
    t, kdim = a.shape
    n = w.shape[1] if n is None else n
    tm = _pick(t, (1024, 512, 256, 128))
    tn = _pick(n, (1024, 768, 512, 384, 256, 128))
    return pl.pallas_call(
        _proj_kernel,
        grid=(t // tm, n // tn),
        in_specs=[pl.BlockSpec((tm, kdim), lambda i, j: (i, 0)),
                  pl.BlockSpec((kdim, tn), lambda i, j: (0, j))],
        out_specs=pl.BlockSpec((tm, tn), lambda i, j: (i, j)),
        out_shape=jax.ShapeDtypeStruct((t, n), out_dtype),
        compiler_params=_cparams("arbitrary", "arbitrary"),
        name="proj",
    )(a, w)


def _rope128(v, c, s):
    return v * c + pltpu.roll(v, 64, axis=1) * s


def _mla_q_kernel(c_ref, gn_ref, w_ref, cos_ref, sin_ref, q_ref, *, heads_per_tile, scale):
    n = _rms_norm(c_ref[...], gn_ref[...]).astype(BF16)
    q = jnp.dot(n, w_ref[...], preferred_element_type=F32) * scale
    c = cos_ref[...]
    s = sin_ref[...]
    for h in range(heads_per_tile):
        o = h * 2 * LANES
        q_ref[:, o:o + LANES] = q[:, o:o + LANES].astype(q_ref.dtype)
        q_ref[:, o + LANES:o + 2 * LANES] = _rope128(q[:, o + LANES:o + 2 * LANES], c, s).astype(q_ref.dtype)


def mla_q_proj(h_main, col_block, rank, gnorm, w, cos_t, sin_t, scale):
    t = h_main.shape[0]
    n = w.shape[1]
    tm = _pick(t, (512, 256, 128))
    tn = _pick(n, (1024, 512, 256))
    return pl.pallas_call(
        functools.partial(_mla_q_kernel, heads_per_tile=tn // (2 * LANES), scale=scale),
        grid=(t // tm, n // tn),
        in_specs=[pl.BlockSpec((tm, rank), lambda i, j: (i, col_block)),
                  pl.BlockSpec((1, rank), lambda i, j: (0, 0)),
                  pl.BlockSpec((rank, tn), lambda i, j: (0, j)),
                  pl.BlockSpec((tm, LANES), lambda i, j: (i, 0)),
                  pl.BlockSpec((tm, LANES), lambda i, j: (i, 0))],
        out_specs=pl.BlockSpec((tm, tn), lambda i, j: (i, j)),
        out_shape=jax.ShapeDtypeStruct((t, n), BF16),
        compiler_params=_cparams("arbitrary", "arbitrary"),
        name="mla_q_proj",
    )(h_main, gnorm.reshape(1, rank), w, cos_t, sin_t)


def _mla_kv_kernel(c_ref, gn_ref, w_ref, o_ref):
    n = _rms_norm(c_ref[...], gn_ref[...]).astype(BF16)
    o_ref[...] = jnp.dot(n, w_ref[...], preferred_element_type=F32).astype(o_ref.dtype)


def mla_kv_proj(h_main, col_block, rank, gnorm, w):
    t = h_main.shape[0]
    n = w.shape[1]
    tm = _pick(t, (512, 256, 128))
    tn = _pick(n, (1024, 512, 256, 128))
    return pl.pallas_call(
        _mla_kv_kernel,
        grid=(t // tm, n // tn),
        in_specs=[pl.BlockSpec((tm, rank), lambda i, j: (i, col_block)),
                  pl.BlockSpec((1, rank), lambda i, j: (0, 0)),
                  pl.BlockSpec((rank, tn), lambda i, j: (0, j))],
        out_specs=pl.BlockSpec((tm, tn), lambda i, j: (i, j)),
        out_shape=jax.ShapeDtypeStruct((t, n), BF16),
        compiler_params=_cparams("arbitrary", "arbitrary"),
        name="mla_kv_proj",
    )(h_main, gnorm.reshape(1, rank), w)


def _ab_small_kernel(hs_ref, cos_ref, sin_ref, wgu_ref, bg_ref, kpe_ref, la_ref):
    hs = hs_ref[...]
    kpe_ref[...] = _rope128(hs[:, :LANES], cos_ref[...], sin_ref[...]).astype(kpe_ref.dtype)
    z = jnp.dot(hs[:, LANES:].astype(BF16), wgu_ref[...], preferred_element_type=F32) + bg_ref[...]
    log_sig = jnp.minimum(z, 0.0) - jnp.log1p(jnp.exp(-jnp.abs(z)))
    la_ref[...] = log_sig / GLA_GATE_NORM


def ab_small(hs, cos_t, sin_t, wgu_pad, b_gate):
    t = hs.shape[0]
    n = wgu_pad.shape[1]
    tm = _pick(t, (512, 256, 128))
    return pl.pallas_call(
        _ab_small_kernel,
        grid=(t // tm,),
        in_specs=[pl.BlockSpec((tm, 2 * LANES), lambda i: (i, 0)),
                  pl.BlockSpec((tm, LANES), lambda i: (i, 0)),
                  pl.BlockSpec((tm, LANES), lambda i: (i, 0)),
                  pl.BlockSpec((LANES, n), lambda i: (0, 0)),
                  pl.BlockSpec((1, n), lambda i: (0, 0))],
        out_specs=[pl.BlockSpec((tm, LANES), lambda i: (i, 0)),
                   pl.BlockSpec((tm, n), lambda i: (i, 0))],
        out_shape=[jax.ShapeDtypeStruct((t, LANES), BF16), jax.ShapeDtypeStruct((t, n), F32)],
        compiler_params=_cparams("arbitrary"),
        name="ab_small",
    )(hs, cos_t, sin_t, wgu_pad, b_gate.reshape(1, n))


ATTN_HEADS = 4


def _mla_attn_kernel(q_ref, kn_ref, kpe_ref, v_ref, o_ref, m_ref, l_ref, acc_ref, *, tq, nh):
    qi = pl.program_id(2)
    heads = range(nh)
    qs = [q_ref[:, h * 2 * LANES:(h + 1) * 2 * LANES] for h in heads]
    m_ref[...] = jnp.full_like(m_ref, -jnp.inf)
    l_ref[...] = jnp.zeros_like(l_ref)
    acc_ref[...] = jnp.zeros_like(acc_ref)

    def step(ki, masked):
        rows = pl.ds(pl.multiple_of(ki * tq, tq), tq)
        kpe = kpe_ref[rows, :]
        ss = [lax.dot_general(qs[h], jnp.concatenate([kn_ref[rows, h * LANES:(h + 1) * LANES], kpe], axis=1),
                              NT, preferred_element_type=F32) for h in heads]
        if masked:
            row = lax.broadcasted_iota(jnp.int32, (tq, tq), 0)
            col = lax.broadcasted_iota(jnp.int32, (tq, tq), 1)
            ss = [jnp.where(col <= row, s, -jnp.inf) for s in ss]
        m_prev = [m_ref[h] for h in heads]
        m_new = [jnp.maximum(m_prev[h], jnp.max(ss[h], axis=-1, keepdims=True)) for h in heads]
        ps = [jnp.exp(ss[h] - jnp.concatenate([m_new[h]] * (tq // LANES), axis=1)) for h in heads]
        alpha = [jnp.exp(m_prev[h] - m_new[h]) for h in heads]
        pv = [jnp.dot(ps[h].astype(BF16), v_ref[rows, h * LANES:(h + 1) * LANES],
                      preferred_element_type=F32) for h in heads]
        for h in heads:
            l_ref[h] = alpha[h] * l_ref[h] + jnp.sum(ps[h], axis=-1, keepdims=True)
            acc_ref[h] = alpha[h] * acc_ref[h] + pv[h]
            m_ref[h] = m_new[h]

    def body(ki, carry):
        step(ki, False)
        return carry

    lax.fori_loop(0, qi, body, 0)
    step(qi, True)
    for h in heads:
        o_ref[:, h * LANES:(h + 1) * LANES] = (acc_ref[h] / l_ref[h]).astype(o_ref.dtype)


def mla_attention(q, kv, kpe, batch, heads):
    t = q.shape[0]
    s = t // batch
    tq = _pick(s, (512, 256, 128))
    nq = s // tq
    nh = _pick(heads, (ATTN_HEADS, 2, 1))
    ng = heads // nh
    return pl.pallas_call(
        functools.partial(_mla_attn_kernel, tq=tq, nh=nh),
        grid=(batch, ng, nq),
        in_specs=[
            pl.BlockSpec((tq, nh * 2 * LANES), lambda b, g, qi: (b * nq + qi, g)),
            pl.BlockSpec((s, nh * LANES), lambda b, g, qi: (b, g)),
            pl.BlockSpec((s, LANES), lambda b, g, qi: (b, 0)),
            pl.BlockSpec((s, nh * LANES), lambda b, g, qi: (b, ng + g)),
        ],
        out_specs=pl.BlockSpec((tq, nh * LANES), lambda b, g, qi: (b * nq + qi, g)),
        out_shape=jax.ShapeDtypeStruct((t, heads * LANES), BF16),
        scratch_shapes=[pltpu.VMEM((nh, tq, LANES), F32), pltpu.VMEM((nh, tq, LANES), F32),
                        pltpu.VMEM((nh, tq, LANES), F32)],
        compiler_params=_cparams("arbitrary", "arbitrary", "arbitrary"),
        name="mla_attention",
    )(q, kv, kpe, kv)


GLA_CHUNKS = 4


def _gla_kernel(q_ref, k_ref, v_ref, la_ref, r_ref, gn_ref, o_ref, st_ref, *, dk, nc):
    c = pl.program_id(2)

    @pl.when(c == 0)
    def _():
        st_ref[...] = jnp.zeros_like(st_ref)

    rows = nc * CHUNK
    ri = lax.broadcasted_iota(jnp.int32, (rows, rows), 0)
    ci = lax.broadcasted_iota(jnp.int32, (rows, rows), 1)
    same_chunk = (ri >> CHUNK_SHIFT) == (ci >> CHUNK_SHIFT)
    tril = jnp.logical_and(same_chunk, ci <= ri)
    la = la_ref[...]
    bcum = _dot_mask(tril.astype(BF16), la)
    b_end = _dot_mask(same_chunk.astype(BF16), la)
    q_in = (q_ref[...] * (dk ** -0.5) * jnp.exp(bcum)).astype(BF16)
    k = k_ref[...]
    v = v_ref[...].astype(BF16)
    k_in = k * jnp.exp(-bcum)
    k_tail = (k * jnp.exp(b_end - bcum)).astype(BF16)
    attn = jnp.where(tril, _dot(q_in, k_in, NT), 0.0)
    o_intra = _dot(attn, v)
    decay = jnp.exp(b_end)
    gn = gn_ref[...]
    st = st_ref[...]
    for j in range(nc):
        sl = slice(j * CHUNK, (j + 1) * CHUNK)
        o = o_intra[sl] + _dot(q_in[sl], st, NT)
        st = st * decay[j * CHUNK:j * CHUNK + 1, :] + _dot(v[sl], k_tail[sl], TN)
        o_ref[sl, :] = (_rms_norm(o, gn) * _silu(r_ref[sl, :])).astype(o_ref.dtype)
    st_ref[...] = st


def gla_mixer(h_main, log_a, gnorm, batch, heads, dk, dv, q_off, k_off, v_off, r_off):
    t = h_main.shape[0]
    nc = _pick(t // batch // CHUNK, (GLA_CHUNKS, 2, 1))
    rows = nc * CHUNK
    n = t // batch // rows
    return pl.pallas_call(
        functools.partial(_gla_kernel, dk=dk, nc=nc),
        grid=(batch, heads, n),
        in_specs=[
            pl.BlockSpec((rows, dk), lambda b, h, c: (b * n + c, q_off // dk + h)),
            pl.BlockSpec((rows, dk), lambda b, h, c: (b * n + c, k_off // dk + h)),
            pl.BlockSpec((rows, dv), lambda b, h, c: (b * n + c, v_off // dv + h)),
            pl.BlockSpec((rows, dk), lambda b, h, c: (b * n + c, h)),
            pl.BlockSpec((rows, dv), lambda b, h, c: (b * n + c, r_off // dv + h)),
            pl.BlockSpec((1, dv), lambda b, h, c: (0, 0)),
        ],
        out_specs=pl.BlockSpec((rows, dv), lambda b, h, c: (b * n + c, h)),
        out_shape=jax.ShapeDtypeStruct((t, heads * dv), BF16),
        scratch_shapes=[pltpu.VMEM((dv, dk), F32)],
        compiler_params=_cparams("arbitrary", "arbitrary", "arbitrary"),
        name="gla_mixer",
    )(h_main, h_main, h_main, log_a, h_main, gnorm.reshape(1, dv))


CONV_ROWS = 512


def _conv_kernel(x_ref, w_ref, o_ref, *, seq, conv_k, n_qk_blocks, q_blocks, q_scale, tc):
    j = pl.program_id(1)
    w = w_ref[...]
    rows = min(CONV_ROWS, seq)

    def conv_chunk(r0, first):
        cur = x_ref[pl.ds(r0, rows), :]
        if first:
            prev = jnp.zeros((8, tc), F32)
        else:
            prev = x_ref[pl.ds(r0 - 8, 8), :]
        ext = jnp.concatenate([prev, cur], axis=0)
        acc = cur * w[conv_k - 1:conv_k, :]
        for d in range(1, conv_k):
            acc = acc + pltpu.roll(ext, d, axis=0)[8:, :] * w[conv_k - 1 - d:conv_k - d, :]
        return _silu(acc)

    def l2n(y):
        parts = []
        for g in range(tc // LANES):
            yg = y[:, g * LANES:(g + 1) * LANES]
            parts.append(yg * lax.rsqrt(jnp.sum(yg * yg, axis=-1, keepdims=True) + L2_EPS))
        return jnp.concatenate(parts, axis=1) if len(parts) > 1 else parts[0]

    for ci in range(seq // rows):
        r0 = ci * rows
        y = conv_chunk(r0, ci == 0)

        @pl.when(j < q_blocks)
        def _():
            o_ref[pl.ds(r0, rows), :] = l2n(y) * q_scale

        @pl.when(jnp.logical_and(j >= q_blocks, j < n_qk_blocks))
        def _():
            o_ref[pl.ds(r0, rows), :] = l2n(y)

        @pl.when(j >= n_qk_blocks)
        def _():
            o_ref[pl.ds(r0, rows), :] = y


def gdn_conv(hc, conv_w, batch, qk_width, conv_ch, q_scale):
    t = hc.shape[0]
    seq = t // batch
    conv_k = conv_w.shape[0]
    tc = 256
    return pl.pallas_call(
        functools.partial(_conv_kernel, seq=seq, conv_k=conv_k, n_qk_blocks=2 * qk_width // tc,
                          q_blocks=qk_width // tc, q_scale=q_scale, tc=tc),
        grid=(batch, conv_ch // tc),
        in_specs=[pl.BlockSpec((seq, tc), lambda b, j: (b, j)),
                  pl.BlockSpec((conv_k, tc), lambda b, j: (0, j))],
        out_specs=pl.BlockSpec((seq, tc), lambda b, j: (b, j)),
        out_shape=jax.ShapeDtypeStruct((t, conv_ch), F32),
        compiler_params=_cparams("arbitrary", "arbitrary"),
        name="gdn_conv",
    )(hc, conv_w)


def _gdn_gates_kernel(hs_ref, negA_ref, dt_ref, o_ref, *, nh):
    hs = hs_ref[...]
    beta = jax.nn.sigmoid(hs)
    z = pltpu.roll(hs, LANES - nh, axis=1) + dt_ref[...]
    sp = jnp.maximum(z, 0.0) + jnp.log1p(jnp.exp(-jnp.abs(z)))
    g = negA_ref[...] * sp
    lane = lax.broadcasted_iota(jnp.int32, hs.shape, 1)
    o_ref[...] = jnp.where(lane < nh, beta, pltpu.roll(g, nh, axis=1))


def gdn_gates(hs, a_log, dt_bias):
    t = hs.shape[0]
    nh = a_log.shape[0]
    tm = _pick(t, (1024, 512, 256, 128))
    pad = LANES - nh
    neg_a = jnp.pad(-jnp.exp(a_log.astype(F32)), (0, pad)).reshape(1, LANES)
    dt = jnp.pad(dt_bias.astype(F32), (0, pad)).reshape(1, LANES)
    return pl.pallas_call(
        functools.partial(_gdn_gates_kernel, nh=nh),
        grid=(t // tm,),
        in_specs=[pl.BlockSpec((tm, LANES), lambda i: (i, 0)),
                  pl.BlockSpec((1, LANES), lambda i: (0, 0)),
                  pl.BlockSpec((1, LANES), lambda i: (0, 0))],
        out_specs=pl.BlockSpec((tm, LANES), lambda i: (i, 0)),
        out_shape=jax.ShapeDtypeStruct((t, LANES), F32),
        compiler_params=_cparams("arbitrary"),
        name="gdn_gates",
    )(hs, neg_a, dt)


GDN_GROUP = 32
INV_BLOCK_SHIFTS = (3, 4, 5)


def _unit_lower_inverses(ms, same_blk, eye):
    m0 = [jnp.where(same_blk[0], m, 0.0) for m in ms]
    m2 = [_dot(x, x) for x in m0]
    a = [eye - x for x in m0]
    t = [_dot(x, y) for x, y in zip(a, m2)]
    m4 = [_dot(x, x) for x in m2]
    a = [x + y for x, y in zip(a, t)]
    t = [_dot(x, y) for x, y in zip(a, m4)]
    a = [x + y for x, y in zip(a, t)]
    inner = same_blk[0]
    for outer in same_blk[1:] + [None]:
        if outer is None:
            sel = jnp.logical_not(inner)
        else:
            sel = jnp.logical_and(outer, jnp.logical_not(inner))
            inner = outer
        ea = [_dot(jnp.where(sel, m, 0.0), x) for m, x in zip(ms, a)]
        t = [_dot(x, y) for x, y in zip(a, ea)]
        a = [x - y for x, y in zip(a, t)]
    return a


def _gdn_kernel(q_ref, k_ref, v_ref, z_ref, gcol_ref, grow_ref, gn_ref, o_ref, st_ref,
                *, nh, dk, dv, rep):
    c = pl.program_id(2)

    @pl.when(c == 0)
    def _():
        st_ref[...] = jnp.zeros_like(st_ref)

    ri = lax.broadcasted_iota(jnp.int32, (CHUNK, CHUNK), 0)
    ci = lax.broadcasted_iota(jnp.int32, (CHUNK, CHUNK), 1)
    incl = ci <= ri
    strict = ci < ri
    gcol = gcol_ref[...]
    grow = grow_ref[0]
    gc_cols = _dot_mask(incl.astype(BF16), gcol)
    gc_rows = _dot_mask((ri <= ci).astype(BF16), grow, mask_first=False)
    gnorm = gn_ref[...]
    same_blk = [(ri >> sh) == (ci >> sh) for sh in INV_BLOCK_SHIFTS]
    eye = (ri == ci).astype(F32)

    heads = range(nh)

    ks = [k_ref[:, kh * dk:(kh + 1) * dk] for kh in range(nh // rep)]
    qs = [q_ref[:, kh * dk:(kh + 1) * dk] for kh in range(nh // rep)]
    kk = [_dot(k, k, NT) for k in ks]
    qk = [_dot(q, k, NT) for q, k in zip(qs, ks)]

    beta = [gcol[:, hh:hh + 1] for hh in heads]
    gc_c = [gc_cols[:, nh + hh:nh + hh + 1] for hh in heads]
    gc_r = [gc_rows[nh + hh:nh + hh + 1, :] for hh in heads]
    gc_last = [g[CHUNK - 1:CHUNK, :] for g in gc_c]
    egc = [jnp.exp(g) for g in gc_c]
    decay = [jnp.exp(jnp.where(incl, gc_c[hh] - gc_r[hh], -jnp.inf)) for hh in heads]
    ms = [jnp.where(strict, beta[hh] * kk[hh // rep] * decay[hh], 0.0) for hh in heads]
    rhs = [jnp.concatenate([v_ref[:, hh * dv:(hh + 1) * dv] * beta[hh],
                            ks[hh // rep] * (beta[hh] * egc[hh])], axis=1) for hh in heads]
    inv = _unit_lower_inverses(ms, same_blk, eye)
    x = [_dot(a, r) for a, r in zip(inv, rhs)]
    st = [st_ref[hh] for hh in heads]
    ws = [_dot(x[hh][:, dv:], st[hh]) for hh in heads]
    o_st = [_dot(qs[hh // rep] * egc[hh], st[hh]) for hh in heads]
    v_new = [x[hh][:, :dv] - ws[hh] for hh in heads]
    o_in = [_dot(qk[hh // rep] * decay[hh], v_new[hh]) for hh in heads]
    kv = [_dot(ks[hh // rep] * jnp.exp(gc_last[hh] - gc_c[hh]), v_new[hh], TN) for hh in heads]
    for hh in heads:
        st_ref[hh] = jnp.exp(gc_last[hh]) * st[hh] + kv[hh]
        zz = z_ref[:, hh * dv:(hh + 1) * dv]
        o = o_st[hh] + o_in[hh]
        o_ref[:, hh * dv:(hh + 1) * dv] = (_rms_norm(o, gnorm) * _silu(zz)).astype(o_ref.dtype)


def gdn_mixer(qkv, hc, gates, gates_t, gnorm, batch, nh, dk, dv, rep, z_off):
    t = qkv.shape[0]
    n = t // batch // CHUNK
    grp = min(GDN_GROUP, nh)
    ngrp = nh // grp
    kw = (grp // rep) * dk
    vw = grp * dv
    nkh = nh // rep
    k_blk0 = nkh * dk // kw
    v_blk0 = 2 * nkh * dk // vw
    z_blk0 = z_off // vw
    return pl.pallas_call(
        functools.partial(_gdn_kernel, nh=grp, dk=dk, dv=dv, rep=rep),
        grid=(batch, ngrp, n),
        in_specs=[
            pl.BlockSpec((CHUNK, kw), lambda b, g, c: (b * n + c, g)),
            pl.BlockSpec((CHUNK, kw), lambda b, g, c: (b * n + c, k_blk0 + g)),
            pl.BlockSpec((CHUNK, vw), lambda b, g, c: (b * n + c, v_blk0 + g)),
            pl.BlockSpec((CHUNK, vw), lambda b, g, c: (b * n + c, z_blk0 + g)),
            pl.BlockSpec((CHUNK, LANES), lambda b, g, c: (b * n + c, g)),
            pl.BlockSpec((1, LANES, CHUNK), lambda b, g, c: (b * n + c, g, 0)),
            pl.BlockSpec((1, dv), lambda b, g, c: (0, 0)),
        ],
        out_specs=pl.BlockSpec((CHUNK, vw), lambda b, g, c: (b * n + c, g)),
        out_shape=jax.ShapeDtypeStruct((t, nh * dv), BF16),
        scratch_shapes=[pltpu.VMEM((grp, dk, dv), F32)],
        compiler_params=_cparams("arbitrary", "arbitrary", "arbitrary"),
        name="gdn_mixer",
    )(qkv, qkv, qkv, hc, gates, gates_t, gnorm.reshape(1, dv))


def _xattn_kernel(q_ref, kv_ref, o_ref, *, heads, dim):
    scale = dim ** -0.5
    for h in range(heads):
        q = (q_ref[:, h * dim:(h + 1) * dim] * scale).astype(BF16)
        k = kv_ref[:, h * dim:(h + 1) * dim]
        v = kv_ref[:, (heads + h) * dim:(heads + h + 1) * dim]
        s = lax.dot_general(q, k, NT, preferred_element_type=F32)
        s = s - jnp.max(s, axis=-1, keepdims=True)
        p = jnp.exp(s)
        p = p / jnp.sum(p, axis=-1, keepdims=True)
        o_ref[:, h * dim:(h + 1) * dim] = jnp.dot(p.astype(BF16), v,
                                                  preferred_element_type=F32).astype(o_ref.dtype)


def cross_attention(q, kv, batch, heads, dim):
    t = q.shape[0]
    s = t // batch
    m = kv.shape[0] // batch
    tq = _pick(s, (512, 256, 128))
    nq = s // tq
    return pl.pallas_call(
        functools.partial(_xattn_kernel, heads=heads, dim=dim),
        grid=(batch, nq),
        in_specs=[pl.BlockSpec((tq, heads * dim), lambda b, i: (b * nq + i, 0)),
                  pl.BlockSpec((m, 2 * heads * dim), lambda b, i: (b, 0))],
        out_specs=pl.BlockSpec((tq, heads * dim), lambda b, i: (b * nq + i, 0)),
        out_shape=jax.ShapeDtypeStruct((t, heads * dim), BF16),
        compiler_params=_cparams("arbitrary", "arbitrary"),
        name="cross_attention",
    )(q, kv)


def _rope_cols(w, half):
    x1, x2 = w[:, :half], w[:, half:2 * half]
    return jnp.concatenate([x1, x2, x2, x1], axis=1)


def kernel(x, mem, positions, mem_ln_g, mem_ln_b, ln_g, ln_b, ffn_w_gate, ffn_w_up, ffn_w_down,
           ab_w_in, mla_q_norm, mla_w_uq, mla_kv_norm, mla_w_ukv, gla_w_gate_up, gla_b_gate, gla_norm,
           ab_w_out, c_w_in, gdn_conv_w, gdn_a_log, gdn_dt_bias, gdn_norm, c_w_out,
           xa_wq, xa_wkv, xa_wo):
    batch, seq, d = x.shape
    t = batch * seq
    mem_len = mem.shape[1]
    depth = ln_g.shape[0]

    q_lora = mla_q_norm.shape[1]
    kv_lora = mla_kv_norm.shape[1]
    gla_dv = gla_norm.shape[1]
    gla_dk = gla_dv // 2
    gla_qk = gla_w_gate_up.shape[2]
    gla_heads = gla_qk // gla_dk
    gate_rank = gla_w_gate_up.shape[1]
    mla_heads = (ab_w_out.shape[1] - gla_heads * gla_dv) // LANES
    nope = LANES
    rope = mla_w_uq.shape[2] // mla_heads - nope
    half = rope // 2
    assert rope == 64 and mla_w_ukv.shape[2] == mla_heads * 2 * LANES
    gdn_vh = gdn_a_log.shape[1]
    gdn_dv = gdn_norm.shape[1]
    gdn_dk = gdn_dv
    gdn_vw = gdn_vh * gdn_dv
    gdn_conv_ch = gdn_conv_w.shape[2]
    gdn_qk = (gdn_conv_ch - gdn_vw) // 2
    gdn_kh = gdn_qk // gdn_dk
    xa_dim = LANES
    xa_heads = xa_wq.shape[2] // xa_dim

    inv = 1.0 / (ROPE_THETA ** (jnp.arange(0, rope, 2, dtype=F32) / rope))
    ang = positions.astype(F32).reshape(t, 1) * inv
    cos, sin = jnp.cos(ang), jnp.sin(ang)
    zeros = jnp.zeros((t, 2 * half), F32)
    cos_t = jnp.concatenate([cos, cos, zeros], axis=1)
    sin_t = jnp.concatenate([-sin, sin, zeros], axis=1)

    memn = ln_rows(mem.reshape(batch * mem_len, d), mem_ln_g, mem_ln_b)

    xf = x.reshape(t, d)
    xb = xf.astype(BF16)

    d_ff = ffn_w_gate.shape[-1]
    f_pad = -(-d_ff // FFN_K_TILE) * FFN_K_TILE

    wd_all = cast_pad_rows(ffn_w_down, f_pad)

    def ffn(xf, xb, l, s):
        h = ffn_gateup(xb, ffn_w_gate, ffn_w_up, l, s, f_pad)
        return down_ln(h, wd_all, xf, ln_g[l, 3 * s], ln_b[l, 3 * s], 0.5, w_idx=(l, s))

    for l in range(depth):
        i = l // 2
        xf, xb = ffn(xf, xb, l, 0)

        if l % 2 == 0:
            w_in = ab_w_in[i].astype(BF16)
            o = 0
            segs = []
            for sz in (q_lora, kv_lora, rope, gla_qk, gla_qk, gla_heads * gla_dv, gate_rank,
                       gla_heads * gla_dv):
                segs.append(w_in[:, o:o + sz])
                o += sz
            w_cq, w_ckv, w_kpe, w_gq, w_gk, w_gv, w_glr, w_r = segs
            w_main = jnp.concatenate([w_cq, w_ckv, w_gq, w_gk, w_gv, w_r], axis=1)
            w_small = jnp.concatenate(
                [_rope_cols(w_kpe, half), w_glr, jnp.zeros((d, LANES - gate_rank), BF16)], axis=1)
            h_main = proj(xb, w_main)
            h_small = proj(xb, w_small)
            q_off = q_lora + kv_lora
            k_off = q_off + gla_qk
            v_off = k_off + gla_qk
            r_off = v_off + gla_heads * gla_dv

            wgu_pad = jnp.pad(gla_w_gate_up[i], ((0, LANES - gate_rank), (0, 0))).astype(BF16)
            kpe, log_a = ab_small(h_small, cos_t, sin_t, wgu_pad, gla_b_gate[i])

            wq = mla_w_uq[i].reshape(q_lora, mla_heads, nope + rope)
            wq = jnp.concatenate(
                [wq[:, :, :nope], wq[:, :, nope:nope + half], wq[:, :, nope + half:],
                 wq[:, :, nope + half:], wq[:, :, nope:nope + half]], axis=2)
            wq = wq.reshape(q_lora, mla_heads * 2 * LANES).astype(BF16)
            q = mla_q_proj(h_main, 0, q_lora, mla_q_norm[i], wq, cos_t, sin_t,
                           float((nope + rope) ** -0.5))
            wkv = mla_w_ukv[i].reshape(kv_lora, mla_heads, 2, LANES)
            wkv = jnp.transpose(wkv, (0, 2, 1, 3)).reshape(kv_lora, 2 * mla_heads * LANES).astype(BF16)
            kv = mla_kv_proj(h_main, q_lora // kv_lora, kv_lora, mla_kv_norm[i], wkv)
            o_mla = mla_attention(q, kv, kpe, batch, mla_heads)

            o_gla = gla_mixer(h_main, log_a, gla_norm[i], batch, gla_heads, gla_dk, gla_dv,
                              q_off, k_off, v_off, r_off)
            y_in = jnp.concatenate([o_mla, o_gla], axis=1)
            w_out = ab_w_out[i].astype(BF16)
        else:
            w_in = c_w_in[i].astype(BF16)
            main_w = gdn_conv_ch + gdn_vw
            hc = proj(xb, w_in, n=main_w)
            w_small = jnp.pad(w_in[:, main_w:], ((0, 0), (0, LANES - 2 * gdn_vh)))
            h_small = proj(xb, w_small)
            qkv = gdn_conv(hc, gdn_conv_w[i], batch, gdn_qk, gdn_conv_ch, float(gdn_dk ** -0.5))
            gates = gdn_gates(h_small, gdn_a_log[i], gdn_dt_bias[i])
            grp = min(GDN_GROUP, gdn_vh)
            ngrp = gdn_vh // grp
            beta_g = gates[:, :gdn_vh].reshape(t, ngrp, grp)
            g_g = gates[:, gdn_vh:2 * gdn_vh].reshape(t, ngrp, grp)
            gg = jnp.concatenate(
                [beta_g, g_g, jnp.zeros((t, ngrp, LANES - 2 * grp), F32)], axis=2)
            gates_c = gg.reshape(t, ngrp * LANES)
            gates_t = jnp.transpose(gg.reshape(t // CHUNK, CHUNK, ngrp * LANES), (0, 2, 1))
            y_in = gdn_mixer(qkv, hc, gates_c, gates_t, gdn_norm[i], batch, gdn_vh, gdn_dk, gdn_dv,
                             gdn_vh // gdn_kh, gdn_conv_ch)
            w_out = c_w_out[i].astype(BF16)

        xf, xb = down_ln(y_in, w_out, xf, ln_g[l, 1], ln_b[l, 1], 1.0)

        qx = proj(xb, xa_wq[l].astype(BF16))
        kvx = proj(memn, xa_wkv[l].astype(BF16), out_dtype=BF16)
        ox = cross_attention(qx, kvx, batch, xa_heads, xa_dim)
        xf, xb = down_ln(ox, xa_wo[l].astype(BF16), xf, ln_g[l, 2], ln_b[l, 2], 1.0)

        xf, xb = ffn(xf, xb, l, 1)

    return xf.reshape(batch, seq, d)
```
